```python
import math
import jax, jax.numpy as jnp
from jax import lax
import numpy as np

D_MODEL = 1024
BATCH = 16
SEQ = 2048
DEPTH = 1

CHUNK = 64
Q_BLOCK = 128
ATT_HEADS = 4
ATT_HEAD_DIM = 64
ATT_V_DIM = 2 * ATT_HEAD_DIM
ATT_WIDTH = ATT_HEADS * ATT_V_DIM
CONV_WIDTH = D_MODEL - ATT_WIDTH
CONV_GROUPS = 8
CONV_K = 3
MIX_WIDTH = ATT_WIDTH + CONV_WIDTH
IN_COLS = 3 * ATT_WIDTH + 3 * CONV_WIDTH
PEER_HEADS = 8
PEER_KEY_DIM = 256
PEER_HALF = PEER_KEY_DIM // 2
N_KEYS = 128
N_EXPERTS = N_KEYS * N_KEYS
PEER_TOPK = 16
TOKEN_BLOCK = 128
EPS = 1e-6

kernel_name = "hymba_conv_diffattn_peer_block"


def rmsnorm(x, g):
    xf = x.astype(jnp.float32)
    y = xf * lax.rsqrt(jnp.mean(xf * xf, axis=-1, keepdims=True) + EPS)
    return (y * g.astype(jnp.float32)).astype(x.dtype)


def alibi_slopes():
    h = np.arange(1, ATT_HEADS + 1, dtype=np.float32)
    return jnp.asarray(2.0 ** (-8.0 * h / ATT_HEADS), dtype=jnp.float32)


def diff_attention(q, k, v, lam):
    B, S = q.shape[0], q.shape[1]
    nb = S // Q_BLOCK
    q = (q * (ATT_HEAD_DIM ** -0.5)).reshape(B, nb, Q_BLOCK, ATT_HEADS, 2, ATT_HEAD_DIM)
    q = jnp.moveaxis(q, 1, 0)
    kpos = jnp.arange(S)
    slopes = alibi_slopes()

    def block(args):
        qb, i = args
        qpos = i * Q_BLOCK + jnp.arange(Q_BLOCK)
        s = jnp.einsum('bqhcd,bkhcd->bchqk', qb, k).astype(jnp.float32)
        dist = jnp.abs(qpos[:, None] - kpos[None, :]).astype(jnp.float32)
        bias = -slopes[:, None, None] * dist
        allowed = (kpos[None, :] // CHUNK) <= (qpos[:, None] // CHUNK)
        s = jnp.where(allowed, s + bias, -jnp.inf)
        p = jax.nn.softmax(s, axis=-1)
        a = p[:, 0] - lam * p[:, 1]
        return jnp.einsum('bhqk,bkhd->bqhd', a.astype(v.dtype), v)

    out = lax.map(block, (q, jnp.arange(nb)))
    return jnp.moveaxis(out, 0, 1).reshape(B, S, ATT_HEADS, ATT_V_DIM)


def short_conv(u, w):
    kern = w[:, None, :].astype(u.dtype)
    return lax.conv_general_dilated(
        u, kern, window_strides=(1,), padding=[(CONV_K - 1, 0)],
        dimension_numbers=('NWC', 'WIO', 'NWC'), feature_group_count=CONV_WIDTH)


def peer(h, wq, keys, u_tab, v_tab):
    B, S, D = h.shape
    T = B * S
    ht = h.reshape(T, D)
    q = (ht @ wq).reshape(T, PEER_HEADS, 2, PEER_HALF)
    sc = jnp.einsum('thpd,hpnd->thpn', q, keys).astype(jnp.float32)
    s_top, i_top = lax.top_k(sc, PEER_TOPK)
    cand = s_top[:, :, 0, :, None] + s_top[:, :, 1, None, :]
    cand_idx = i_top[:, :, 0, :, None] * N_KEYS + i_top[:, :, 1, None, :]
    cand = cand.reshape(T, PEER_HEADS, PEER_TOPK * PEER_TOPK)
    cand_idx = cand_idx.reshape(T, PEER_HEADS, PEER_TOPK * PEER_TOPK)
    best, pos = lax.top_k(cand, PEER_TOPK)
    idx = jnp.take_along_axis(cand_idx, pos, axis=-1)
    g = jax.nn.softmax(best, axis=-1)
    HK = PEER_HEADS * PEER_TOPK
    nb = T // TOKEN_BLOCK

    def block(args):
        xb, ib, gb = args
        u = u_tab[ib]
        a = jax.nn.gelu(jnp.einsum('td,ted->te', xb, u), approximate=False)
        v = v_tab[ib]
        return jnp.einsum('te,ted->td', (gb * a).astype(v.dtype), v)

    out = lax.map(block, (ht.reshape(nb, TOKEN_BLOCK, D),
                          idx.reshape(nb, TOKEN_BLOCK, HK),
                          g.reshape(nb, TOKEN_BLOCK, HK)))
    return out.reshape(B, S, D)


def setup_inputs(seed: int = 0) -> dict:
    key = jax.random.key(seed)
    ks = jax.random.split(key, 24)
    f32 = jnp.float32
    L, D = DEPTH, D_MODEL
    nrm = lambda k, shape, s: jax.random.normal(k, shape, f32) * s
    return {
        "x": nrm(ks[0], (BATCH, SEQ, D), 1.0),
        "c": nrm(ks[1], (BATCH, D), 1.0),
        "ada_w": nrm(ks[2], (L, D, 6 * D), 0.5 * D ** -0.5),
        "ada_b": nrm(ks[3], (L, 6 * D), 0.02),
        "norm1_g": 1.0 + nrm(ks[4], (L, D), 0.02),
        "w_in": nrm(ks[5], (L, D, IN_COLS), D ** -0.5),
        "conv_w": nrm(ks[6], (L, CONV_K, CONV_WIDTH), CONV_K ** -0.5),
        "lam_q1": nrm(ks[7], (L, ATT_HEAD_DIM), 0.1),
        "lam_k1": nrm(ks[8], (L, ATT_HEAD_DIM), 0.1),
        "lam_q2": nrm(ks[9], (L, ATT_HEAD_DIM), 0.1),
        "lam_k2": nrm(ks[10], (L, ATT_HEAD_DIM), 0.1),
        "attn_norm_g": 1.0 + nrm(ks[11], (L, ATT_WIDTH), 0.02),
        "conv_norm_g": 1.0 + nrm(ks[12], (L, CONV_WIDTH), 0.02),
        "w_o": nrm(ks[13], (L, MIX_WIDTH, D), MIX_WIDTH ** -0.5),
        "norm2_g": 1.0 + nrm(ks[14], (L, D), 0.02),
        "peer_wq": nrm(ks[15], (L, D, PEER_HEADS * PEER_KEY_DIM), D ** -0.5),
        "peer_keys": nrm(ks[16], (L, PEER_HEADS, 2, N_KEYS, PEER_HALF), PEER_HALF ** -0.5),
        "peer_u": nrm(ks[17], (L, N_EXPERTS, D), D ** -0.5),
        "peer_v": nrm(ks[18], (L, N_EXPERTS, D), PEER_HEADS ** -0.5),
        "final_g": 1.0 + nrm(ks[19], (D,), 0.02),
    }


def reference(x, c, ada_w, ada_b, norm1_g, w_in, conv_w, lam_q1, lam_k1, lam_q2, lam_k2,
              attn_norm_g, conv_norm_g, w_o, norm2_g, peer_wq, peer_keys, peer_u, peer_v,
              final_g):
    B, S, D = x.shape
    split_at = [ATT_WIDTH, 2 * ATT_WIDTH, 3 * ATT_WIDTH,
                3 * ATT_WIDTH + CONV_WIDTH, 3 * ATT_WIDTH + 2 * CONV_WIDTH]
    for l in range(DEPTH):
        mod = jax.nn.silu(c) @ ada_w[l] + ada_b[l]
        sh1, sc1, g1, sh2, sc2, g2 = jnp.split(mod[:, None, :], 6, axis=-1)

        h = rmsnorm(x, norm1_g[l]) * (1.0 + sc1) + sh1
        p = h @ w_in[l]
        q, k, v, cb, cc, ch = jnp.split(p, split_at, axis=-1)
        q = q.reshape(B, S, ATT_HEADS, 2, ATT_HEAD_DIM)
        k = k.reshape(B, S, ATT_HEADS, 2, ATT_HEAD_DIM)
        v = v.reshape(B, S, ATT_HEADS, ATT_V_DIM)

        lam_init = 0.8 - 0.6 * math.exp(-0.3 * l)
        lam = (jnp.exp(jnp.sum(lam_q1[l].astype(jnp.float32) * lam_k1[l].astype(jnp.float32)))
               - jnp.exp(jnp.sum(lam_q2[l].astype(jnp.float32) * lam_k2[l].astype(jnp.float32)))
               + lam_init)
        att = diff_attention(q, k, v, lam)
        att = rmsnorm(att, attn_norm_g[l].reshape(ATT_HEADS, ATT_V_DIM)) * (1.0 - lam_init)
        att = att.reshape(B, S, ATT_WIDTH)

        cv = cb * short_conv(cc * ch, conv_w[l])
        cv = rmsnorm(cv, conv_norm_g[l])

        mix = jnp.concatenate([att, cv], axis=-1) @ w_o[l]
        x = x + g1 * mix

        h2 = rmsnorm(x, norm2_g[l]) * (1.0 + sc2) + sh2
        x = x + g2 * peer(h2, peer_wq[l], peer_keys[l], peer_u[l], peer_v[l])
    return rmsnorm(x, final_g)
```

```python
import functools
import math

import jax
import jax.numpy as jnp
from jax import lax
from jax.experimental import pallas as pl
from jax.experimental.pallas import tpu as pltpu

F32 = jnp.float32
BF16 = jnp.bfloat16

EPS = 1e-6
CHUNK = 64
ATT_HEADS = 4
ATT_HEAD_DIM = 64
ATT_V_DIM = 128
ATT_WIDTH = ATT_HEADS * ATT_V_DIM
CONV_K = 3
PEER_HEADS = 8
N_KEYS = 128
PEER_TOPK = 16
LAM_INIT = 0.8 - 0.6 * math.exp(0.0)
INV_SQRT2 = 0.7071067811865476

LANES = 128
SUBLANES = 8
Q_BLK = 256
TOK_BLK = 512
EXPERT_BLK = 1024
VMEM_LIMIT = 48 * 1024 * 1024


def _cparams(sem):
    return pltpu.CompilerParams(dimension_semantics=sem, vmem_limit_bytes=VMEM_LIMIT)


def _rms(x):
    return x * lax.rsqrt(jnp.mean(x * x, axis=-1, keepdims=True) + EPS)


def _mod_kernel(c_ref, w_ref, b_ref, o_ref):
    sc = jax.nn.silu(c_ref[...])
    o_ref[...] = jnp.dot(sc.astype(BF16), w_ref[...].astype(BF16),
                         preferred_element_type=F32) + b_ref[...]


def _modulation(c, ada_w, ada_b):
    B, D = c.shape
    N = ada_w.shape[1]
    return pl.pallas_call(
        _mod_kernel,
        grid=(N // D,),
        in_specs=[pl.BlockSpec((B, D), lambda j: (0, 0)),
                  pl.BlockSpec((D, D), lambda j: (0, j)),
                  pl.BlockSpec((1, D), lambda j: (0, j))],
        out_specs=pl.BlockSpec((B, D), lambda j: (0, j)),
        out_shape=jax.ShapeDtypeStruct((B, N), F32),
        compiler_params=_cparams(("arbitrary",)),
        name="mod",
    )(c, ada_w, ada_b)


def _inproj_kernel(x_ref, mod_ref, g_ref, w_ref, qkv_ref, cbh_ref):
    y = _rms(x_ref[...])
    h = (y * g_ref[...]) * (1.0 + mod_ref[1:2, :]) + mod_ref[0:1, :]
    p = jnp.dot(h.astype(BF16), w_ref[...], preferred_element_type=F32)
    n_att = qkv_ref.shape[1]
    qkv_ref[...] = p[:, :n_att].astype(BF16)
    cbh_ref[...] = p[:, n_att:]


def _inproj(x2, mod3, g, w, S):
    T, D = x2.shape
    n_att = 3 * ATT_WIDTH
    n_conv = w.shape[1] - n_att
    per_b = S // TOK_BLK
    return pl.pallas_call(
        _inproj_kernel,
        grid=(T // TOK_BLK,),
        in_specs=[pl.BlockSpec((TOK_BLK, D), lambda i: (i, 0)),
                  pl.BlockSpec((None, 6, D), lambda i: (i // per_b, 0, 0)),
                  pl.BlockSpec((1, D), lambda i: (0, 0)),
                  pl.BlockSpec(w.shape, lambda i: (0, 0))],
        out_specs=[pl.BlockSpec((TOK_BLK, n_att), lambda i: (i, 0)),
                   pl.BlockSpec((TOK_BLK, n_conv), lambda i: (i, 0))],
        out_shape=[jax.ShapeDtypeStruct((T, n_att), BF16),
                   jax.ShapeDtypeStruct((T, n_conv), F32)],
        compiler_params=_cparams(("arbitrary",)),
        name="inproj",
    )(x2, mod3, g, w)


def _attn_kernel(q_ref, k_ref, v_ref, lam_ref, g_ref, o_ref):
    S = q_ref.shape[0]
    h = pl.program_id(1)
    slope = jnp.where(h == 0, 2.0 ** -2,
                      jnp.where(h == 1, 2.0 ** -4,
                                jnp.where(h == 2, 2.0 ** -6, 2.0 ** -8))).astype(F32)
    lv = lam_ref[...]
    lam = (jnp.exp(jnp.sum(lv[0:1, :] * lv[1:2, :], axis=-1, keepdims=True))
           - jnp.exp(jnp.sum(lv[2:3, :] * lv[3:4, :], axis=-1, keepdims=True))
           + LAM_INIT)
    gain = g_ref[...] * (1.0 - LAM_INIT)
    nt = (((1,), (1,)), ((), ()))
    for j in range(S // Q_BLK):
        q0 = j * Q_BLK
        L = q0 + Q_BLK
        qb = q_ref[q0:q0 + Q_BLK, :]
        lane = lax.broadcasted_iota(jnp.int32, qb.shape, 1)
        scale = jnp.asarray(ATT_HEAD_DIM ** -0.5, BF16)
        zero = jnp.zeros_like(qb)
        q_lo = jnp.where(lane < ATT_HEAD_DIM, qb, zero) * scale
        q_hi = jnp.where(lane >= ATT_HEAD_DIM, qb, zero) * scale
        kk = k_ref[0:L, :]
        vv = v_ref[0:L, :]
        r = lax.broadcasted_iota(jnp.int32, (Q_BLK, L), 0) + q0
        c = lax.broadcasted_iota(jnp.int32, (Q_BLK, L), 1)
        bias = -slope * jnp.abs(r - c).astype(F32)
        allowed = (c // CHUNK) <= (r // CHUNK)
        bias = jnp.where(allowed, bias, -jnp.inf)

        def half(qh):
            s = lax.dot_general(qh, kk, nt, preferred_element_type=F32) + bias
            m = jnp.max(s, axis=-1, keepdims=True)
            p = jnp.exp(s - m)
            l = jnp.sum(p, axis=-1, keepdims=True)
            o = jnp.dot(p.astype(BF16), vv, preferred_element_type=F32)
            return o / l

        o = half(q_lo) - lam * half(q_hi)
        o_ref[q0:q0 + Q_BLK, :] = (_rms(o) * gain).astype(o_ref.dtype)


def _attention(qkv3, lam_vecs, attn_g):
    B, S, _ = qkv3.shape
    dv = ATT_V_DIM
    blk = lambda off: pl.BlockSpec((None, S, dv), lambda b, h: (b, 0, off + h))
    return pl.pallas_call(
        _attn_kernel,
        grid=(B, ATT_HEADS),
        in_specs=[blk(0), blk(ATT_HEADS), blk(2 * ATT_HEADS),
                  pl.BlockSpec(lam_vecs.shape, lambda b, h: (0, 0)),
                  pl.BlockSpec((1, dv), lambda b, h: (0, h))],
        out_specs=pl.BlockSpec((None, S, dv), lambda b, h: (b, 0, h)),
        out_shape=jax.ShapeDtypeStruct((B, S, ATT_WIDTH), BF16),
        compiler_params=_cparams(("arbitrary", "arbitrary")),
        name="attn",
    )(qkv3, qkv3, qkv3, lam_vecs, attn_g)


def _mix_kernel(x_ref, att_ref, cb_ref, cc_ref, ch_ref, ccp_ref, chp_ref, cw_ref, cg_ref,
                wo_ref, mod_ref, g2_ref, wqt_ref, x1_ref, h2t_ref, qpt_ref, *, blocks_per_seq):
    i = pl.program_id(0)
    u = cc_ref[...] * ch_ref[...]
    first = (i % blocks_per_seq) == 0
    up = jnp.where(first, 0.0, ccp_ref[...] * chp_ref[...])
    row = lax.broadcasted_iota(jnp.int32, up.shape, 0)

    def shifted(k):
        uk = pltpu.roll(u, k, 0)
        top = jnp.where(row < k, pltpu.roll(up, k, 0), uk[0:SUBLANES, :])
        return jnp.concatenate([top, uk[SUBLANES:, :]], axis=0)

    cw = cw_ref[...]
    y = cw[2:3, :] * u + cw[1:2, :] * shifted(1) + cw[0:1, :] * shifted(2)
    cv = _rms(cb_ref[...] * y) * cg_ref[...]
    both = jnp.concatenate([att_ref[...], cv.astype(BF16)], axis=-1)
    mix = jnp.dot(both, wo_ref[...], preferred_element_type=F32)
    x1 = x_ref[...] + mod_ref[2:3, :] * mix
    x1_ref[...] = x1
    h2 = (_rms(x1) * g2_ref[...]) * (1.0 + mod_ref[4:5, :]) + mod_ref[3:4, :]
    h2t = h2.T.astype(BF16)
    h2t_ref[...] = h2t
    qpt_ref[...] = jnp.dot(wqt_ref[...], h2t, preferred_element_type=F32).astype(BF16)


def _mix(x2, att2, cbh, conv_w, conv_g, w_o, mod3, g2, wq_t, S):
    T, D = x2.shape
    C = conv_w.shape[1]
    NQ = wq_t.shape[0]
    per_b = S // TOK_BLK
    rows8 = TOK_BLK // SUBLANES
    prev = lambda col: pl.BlockSpec((SUBLANES, C), lambda i: (jnp.maximum(i * rows8 - 1, 0), col))
    full = lambda a: pl.BlockSpec(a.shape, lambda i: (0,) * a.ndim)
    return pl.pallas_call(
        functools.partial(_mix_kernel, blocks_per_seq=per_b),
        grid=(T // TOK_BLK,),
        in_specs=[pl.BlockSpec((TOK_BLK, D), lambda i: (i, 0)),
                  pl.BlockSpec((TOK_BLK, ATT_WIDTH), lambda i: (i, 0)),
                  pl.BlockSpec((TOK_BLK, C), lambda i: (i, 0)),
                  pl.BlockSpec((TOK_BLK, C), lambda i: (i, 1)),
                  pl.BlockSpec((TOK_BLK, C), lambda i: (i, 2)),
                  prev(1), prev(2),
                  full(conv_w), full(conv_g), full(w_o),
                  pl.BlockSpec((None, 6, D), lambda i: (i // per_b, 0, 0)),
                  full(g2), full(wq_t)],
        out_specs=[pl.BlockSpec((TOK_BLK, D), lambda i: (i, 0)),
                   pl.BlockSpec((D, TOK_BLK), lambda i: (0, i)),
                   pl.BlockSpec((NQ, TOK_BLK), lambda i: (0, i))],
        out_shape=[jax.ShapeDtypeStruct((T, D), F32),
                   jax.ShapeDtypeStruct((D, T), BF16),
                   jax.ShapeDtypeStruct((NQ, T), BF16)],
        compiler_params=_cparams(("arbitrary",)),
        name="mix",
    )(x2, att2, cbh, cbh, cbh, cbh, cbh, conv_w, conv_g, w_o, mod3, g2, wq_t)


_PAIR_ROWS = ((0, 8, 0), (0, 8, 8), (1, 8, 0), (2, 5, 0), (3, 4, 0), (4, 3, 0),
              (5, 2, 0), (6, 2, 0), (7, 2, 0))


def _top_values(x, out_ref, rounds):
    for r in range(rounds):
        m = jnp.max(x, axis=0, keepdims=True)
        out_ref[r:r + 1, :] = m
        x = jnp.where(x == m, -jnp.inf, x)


def _select_kernel(qpt_ref, keys_ref, s1_ref, s2_ref, e1_ref, e2_ref, tau_ref,
                   a_ref, b_ref, t_ref):
    TB = qpt_ref.shape[1]
    for hp in range(2 * PEER_HEADS):
        s = jnp.dot(keys_ref[hp], qpt_ref[hp * N_KEYS:(hp + 1) * N_KEYS, :],
                    preferred_element_type=F32)
        (s1_ref if hp % 2 == 0 else s2_ref)[hp // 2] = s

    row8 = lax.broadcasted_iota(jnp.int32, (SUBLANES, LANES), 0)

    def per_head(h, carry):
        for lt in range(TB // LANES):
            lanes = slice(lt * LANES, (lt + 1) * LANES)
            s1 = s1_ref[h, :, lanes]
            s2 = s2_ref[h, :, lanes]
            _top_values(s1, a_ref, PEER_TOPK)
            _top_values(s2, b_ref, PEER_TOPK)
            a = a_ref[...]
            b = b_ref[...]
            ea = jnp.exp(a - a[0:1, :])
            eb = jnp.exp(b - b[0:1, :])
            cands, prods = [], []
            for (i, n, j0) in _PAIR_ROWS:
                cs = a[i:i + 1, :] + b[j0:j0 + SUBLANES, :]
                ps = ea[i:i + 1, :] * eb[j0:j0 + SUBLANES, :]
                if n < SUBLANES:
                    cs = jnp.where(row8 < n, cs, -jnp.inf)
                cands.append(cs)
                prods.append(ps)
            cands.append(a[SUBLANES:, :] + b[0:1, :])
            prods.append(ea[SUBLANES:, :] * eb[0:1, :])
            cand = jnp.concatenate(cands, axis=0)
            prod = jnp.concatenate(prods, axis=0)
            _top_values(cand, t_ref, PEER_TOPK)
            tau = t_ref[PEER_TOPK - 1:PEER_TOPK, :]
            z = jnp.sum(jnp.where(cand >= tau, prod, 0.0), axis=0, keepdims=True)
            tau_ref[h, :, lanes] = tau
            e1_ref[h, :, lanes] = 0.5 * jnp.exp(s1 - a[0:1, :])
            e2_ref[h, :, lanes] = jnp.exp(s2 - b[0:1, :]) / z
        return carry

    lax.fori_loop(0, PEER_HEADS, per_head, 0)


def _select(qp_t, keys):
    NQ, T = qp_t.shape
    H = PEER_HEADS
    big = pl.BlockSpec((H, N_KEYS, TOK_BLK), lambda i: (0, 0, i))
    big_shape = jax.ShapeDtypeStruct((H, N_KEYS, T), F32)
    return pl.pallas_call(
        _select_kernel,
        grid=(T // TOK_BLK,),
        in_specs=[pl.BlockSpec((NQ, TOK_BLK), lambda i: (0, i)),
                  pl.BlockSpec(keys.shape, lambda i: (0, 0, 0))],
        out_specs=[big, big, big, big, pl.BlockSpec((H, 1, TOK_BLK), lambda i: (0, 0, i))],
        out_shape=[big_shape, big_shape, big_shape, big_shape,
                   jax.ShapeDtypeStruct((H, 1, T), F32)],
        scratch_shapes=[pltpu.VMEM((PEER_TOPK, LANES), F32),
                        pltpu.VMEM((PEER_TOPK, LANES), F32),
                        pltpu.VMEM((PEER_TOPK, LANES), F32)],
        compiler_params=_cparams(("arbitrary",)),
        name="select",
    )(qp_t, keys)


def _expert_kernel(h2t_ref, u_ref, vt_ref, s1_ref, e1_ref, s2_ref, e2_ref, tau_ref,
                   x1_ref, mod_ref, fg_ref, o_ref, acc_ref, a_ref, p_ref):
    c = pl.program_id(1)
    EC, TB = a_ref.shape

    @pl.when(c == 0)
    def _():
        acc_ref[...] = jnp.zeros_like(acc_ref)

    a_ref[...] = jnp.dot(u_ref[...], h2t_ref[...], preferred_element_type=F32)
    for j in range(EC // N_KEYS):
        rows = slice(j * N_KEYS, (j + 1) * N_KEYS)
        for lt in range(TB // LANES):
            lanes = slice(lt * LANES, (lt + 1) * LANES)
            w = jnp.zeros((N_KEYS, LANES), F32)
            for h in range(PEER_HEADS):
                hit = (s1_ref[h, j:j + 1, lanes] + s2_ref[h, :, lanes]) >= tau_ref[h, :, lanes]
                w = w + jnp.where(hit, e1_ref[h, j:j + 1, lanes] * e2_ref[h, :, lanes], 0.0)
            a = a_ref[rows, lanes]
            g = a * (1.0 + lax.erf(a * INV_SQRT2))
            p_ref[rows, lanes] = (w * g).astype(BF16)
    acc_ref[...] += jnp.dot(vt_ref[...], p_ref[...], preferred_element_type=F32)

    @pl.when(c == pl.num_programs(1) - 1)
    def _():
        x2 = x1_ref[...] + mod_ref[5:6, :] * acc_ref[...].T
        o_ref[...] = _rms(x2) * fg_ref[...]


def _experts(h2_t, u_b, v_t, s1, e1, s2, e2, tau, x1, mod3, final_g, S):
    D, T = h2_t.shape
    E = u_b.shape[0]
    H = PEER_HEADS
    TB, EC = TOK_BLK, EXPERT_BLK
    per_b = S // TB
    rows = pl.BlockSpec((H, EC // N_KEYS, TB), lambda i, c: (0, c, i))
    tiles = pl.BlockSpec((H, N_KEYS, TB), lambda i, c: (0, 0, i))
    return pl.pallas_call(
        _expert_kernel,
        grid=(T // TB, E // EC),
        in_specs=[pl.BlockSpec((D, TB), lambda i, c: (0, i)),
                  pl.BlockSpec((EC, D), lambda i, c: (c, 0)),
                  pl.BlockSpec((D, EC), lambda i, c: (0, c)),
                  rows, rows, tiles, tiles,
                  pl.BlockSpec((H, 1, TB), lambda i, c: (0, 0, i)),
                  pl.BlockSpec((TB, D), lambda i, c: (i, 0)),
                  pl.BlockSpec((None, 6, D), lambda i, c: (i // per_b, 0, 0)),
                  pl.BlockSpec((1, D), lambda i, c: (0, 0))],
        out_specs=pl.BlockSpec((TB, D), lambda i, c: (i, 0)),
        out_shape=jax.ShapeDtypeStruct((T, D), F32),
        scratch_shapes=[pltpu.VMEM((D, TB), F32),
                        pltpu.VMEM((EC, TB), F32),
                        pltpu.VMEM((EC, TB), BF16)],
        compiler_params=_cparams(("arbitrary", "arbitrary")),
        name="experts",
    )(h2_t, u_b, v_t, s1, e1, s2, e2, tau, x1, mod3, final_g)


def kernel(x, c, ada_w, ada_b, norm1_g, w_in, conv_w, lam_q1, lam_k1, lam_q2, lam_k2, attn_norm_g, conv_norm_g, w_o, norm2_g, peer_wq, peer_keys, peer_u, peer_v, final_g):
    B, S, D = x.shape
    T = B * S
    assert ada_w.shape[0] == 1, "single-layer block"
    assert S % TOK_BLK == 0 and S % Q_BLK == 0 and Q_BLK % CHUNK == 0
    assert peer_u.shape[1] % EXPERT_BLK == 0 and EXPERT_BLK // N_KEYS == SUBLANES

    x2 = x.reshape(T, D)
    mod3 = _modulation(c, ada_w[0], ada_b).reshape(B, 6, D)

    qkv, cbh = _inproj(x2, mod3, norm1_g, w_in[0].astype(BF16), S)

    lam_vecs = jnp.concatenate([lam_q1, lam_k1, lam_q2, lam_k2], axis=0)
    att = _attention(qkv.reshape(B, S, -1), lam_vecs, attn_norm_g)

    x1, h2_t, qp_t = _mix(x2, att.reshape(T, ATT_WIDTH), cbh, conv_w[0], conv_norm_g,
                          w_o[0].astype(BF16), mod3, norm2_g, peer_wq[0].T.astype(BF16), S)

    keys = peer_keys[0].reshape(2 * PEER_HEADS, N_KEYS, -1).astype(BF16)
    s1, s2, e1, e2, tau = _select(qp_t, keys)

    out = _experts(h2_t, peer_u[0].astype(BF16), peer_v[0].T.astype(BF16),
                   s1, e1, s2, e2, tau, x1, mod3, final_g.reshape(1, D), S)
    return out.reshape(B, S, D)
```

```python
import functools
import math

import jax
import jax.numpy as jnp
from jax import lax
from jax.experimental import pallas as pl
from jax.experimental.pallas import tpu as pltpu

F32 = jnp.float32
BF16 = jnp.bfloat16

EPS = 1e-6
CHUNK = 64
ATT_HEADS = 4
ATT_HEAD_DIM = 64
ATT_V_DIM = 128
ATT_WIDTH = ATT_HEADS * ATT_V_DIM
CONV_K = 3
PEER_HEADS = 8
N_KEYS = 128
PEER_TOPK = 16
LAM_INIT = 0.8 - 0.6 * math.exp(0.0)
INV_SQRT2 = 0.7071067811865476

LANES = 128
SUBLANES = 8
BF16_ROWS = 2 * SUBLANES
Q_BLK = 256
TOK_BLK = 512
EXPERT_BLK = 1024
VMEM_LIMIT = 48 * 1024 * 1024


def _cparams(sem):
    return pltpu.CompilerParams(dimension_semantics=sem, vmem_limit_bytes=VMEM_LIMIT)


def _rms(x):
    return x * lax.rsqrt(jnp.mean(x * x, axis=-1, keepdims=True) + EPS)


def _mod_kernel(c_ref, w_ref, b_ref, o_ref):
    sc = jax.nn.silu(c_ref[...])
    o_ref[...] = jnp.dot(sc.astype(BF16), w_ref[...].astype(BF16),
                         preferred_element_type=F32) + b_ref[...]


def _modulation(c, ada_w, ada_b):
    B, D = c.shape
    N = ada_w.shape[1]
    return pl.pallas_call(
        _mod_kernel,
        grid=(N // D,),
        in_specs=[pl.BlockSpec((B, D), lambda j: (0, 0)),
                  pl.BlockSpec((D, D), lambda j: (0, j)),
                  pl.BlockSpec((1, D), lambda j: (0, j))],
        out_specs=pl.BlockSpec((B, D), lambda j: (0, j)),
        out_shape=jax.ShapeDtypeStruct((B, N), F32),
        compiler_params=_cparams(("arbitrary",)),
        name="mod",
    )(c, ada_w, ada_b)


def _inproj_kernel(x_ref, mod_ref, g_ref, w_ref, qkv_ref, cbh_ref):
    y = _rms(x_ref[...])
    h = (y * g_ref[...]) * (1.0 + mod_ref[1:2, :]) + mod_ref[0:1, :]
    p = jnp.dot(h.astype(BF16), w_ref[...], preferred_element_type=F32)
    n_att = qkv_ref.shape[1]
    qkv_ref[...] = p[:, :n_att].astype(BF16)
    cbh_ref[...] = p[:, n_att:]


def _inproj(x2, mod3, g, w, S):
    T, D = x2.shape
    n_att = 3 * ATT_WIDTH
    n_conv = w.shape[1] - n_att
    per_b = S // TOK_BLK
    return pl.pallas_call(
        _inproj_kernel,
        grid=(T // TOK_BLK,),
        in_specs=[pl.BlockSpec((TOK_BLK, D), lambda i: (i, 0)),
                  pl.BlockSpec((None, 6, D), lambda i: (i // per_b, 0, 0)),
                  pl.BlockSpec((1, D), lambda i: (0, 0)),
                  pl.BlockSpec(w.shape, lambda i: (0, 0))],
        out_specs=[pl.BlockSpec((TOK_BLK, n_att), lambda i: (i, 0)),
                   pl.BlockSpec((TOK_BLK, n_conv), lambda i: (i, 0))],
        out_shape=[jax.ShapeDtypeStruct((T, n_att), BF16),
                   jax.ShapeDtypeStruct((T, n_conv), F32)],
        compiler_params=_cparams(("arbitrary",)),
        name="inproj",
    )(x2, mod3, g, w)


def _attn_kernel(q_ref, k_ref, v_ref, lam_ref, g_ref, o_ref):
    S = q_ref.shape[0]
    h = pl.program_id(1)
    slope = jnp.where(h == 0, 2.0 ** -2,
                      jnp.where(h == 1, 2.0 ** -4,
                                jnp.where(h == 2, 2.0 ** -6, 2.0 ** -8))).astype(F32)
    lv = lam_ref[...]
    lam = (jnp.exp(jnp.sum(lv[0:1, :] * lv[1:2, :], axis=-1, keepdims=True))
           - jnp.exp(jnp.sum(lv[2:3, :] * lv[3:4, :], axis=-1, keepdims=True))
           + LAM_INIT)
    gain = g_ref[...] * (1.0 - LAM_INIT)
    nt = (((1,), (1,)), ((), ()))
    for j in range(S // Q_BLK):
        q0 = j * Q_BLK
        L = q0 + Q_BLK
        qb = q_ref[q0:q0 + Q_BLK, :]
        lane = lax.broadcasted_iota(jnp.int32, qb.shape, 1)
        scale = jnp.asarray(ATT_HEAD_DIM ** -0.5, BF16)
        zero = jnp.zeros_like(qb)
        q_lo = jnp.where(lane < ATT_HEAD_DIM, qb, zero) * scale
        q_hi = jnp.where(lane >= ATT_HEAD_DIM, qb, zero) * scale
        kk = k_ref[0:L, :]
        vv = v_ref[0:L, :]
        r = lax.broadcasted_iota(jnp.int32, (Q_BLK, L), 0) + q0
        c = lax.broadcasted_iota(jnp.int32, (Q_BLK, L), 1)
        bias = -slope * jnp.abs(r - c).astype(F32)
        allowed = (c // CHUNK) <= (r // CHUNK)
        bias = jnp.where(allowed, bias, -jnp.inf)

        def half(qh):
            s = lax.dot_general(qh, kk, nt, preferred_element_type=F32) + bias
            m = jnp.max(s, axis=-1, keepdims=True)
            p = jnp.exp(s - m)
            l = jnp.sum(p, axis=-1, keepdims=True)
            o = jnp.dot(p.astype(BF16), vv, preferred_element_type=F32)
            return o / l

        o = half(q_lo) - lam * half(q_hi)
        o_ref[q0:q0 + Q_BLK, :] = (_rms(o) * gain).astype(o_ref.dtype)


def _attention(qkv3, lam_vecs, attn_g):
    B, S, _ = qkv3.shape
    dv = ATT_V_DIM
    blk = lambda off: pl.BlockSpec((None, S, dv), lambda b, h: (b, 0, off + h))
    return pl.pallas_call(
        _attn_kernel,
        grid=(B, ATT_HEADS),
        in_specs=[blk(0), blk(ATT_HEADS), blk(2 * ATT_HEADS),
                  pl.BlockSpec(lam_vecs.shape, lambda b, h: (0, 0)),
                  pl.BlockSpec((1, dv), lambda b, h: (0, h))],
        out_specs=pl.BlockSpec((None, S, dv), lambda b, h: (b, 0, h)),
        out_shape=jax.ShapeDtypeStruct((B, S, ATT_WIDTH), BF16),
        compiler_params=_cparams(("arbitrary", "arbitrary")),
        name="attn",
    )(qkv3, qkv3, qkv3, lam_vecs, attn_g)


def _mix_kernel(x_ref, att_ref, cb_ref, cc_ref, ch_ref, ccp_ref, chp_ref, cw_ref, cg_ref,
                wo_ref, mod_ref, g2_ref, wqt_ref, x1_ref, h2t_ref, qpt_ref, *, blocks_per_seq):
    i = pl.program_id(0)
    u = cc_ref[...] * ch_ref[...]
    first = (i % blocks_per_seq) == 0
    up = jnp.where(first, 0.0, ccp_ref[...] * chp_ref[...])
    row = lax.broadcasted_iota(jnp.int32, up.shape, 0)

    def shifted(k):
        uk = pltpu.roll(u, k, 0)
        top = jnp.where(row < k, pltpu.roll(up, k, 0), uk[0:SUBLANES, :])
        return jnp.concatenate([top, uk[SUBLANES:, :]], axis=0)

    cw = cw_ref[...]
    y = cw[2:3, :] * u + cw[1:2, :] * shifted(1) + cw[0:1, :] * shifted(2)
    cv = _rms(cb_ref[...] * y) * cg_ref[...]
    both = jnp.concatenate([att_ref[...], cv.astype(BF16)], axis=-1)
    mix = jnp.dot(both, wo_ref[...], preferred_element_type=F32)
    x1 = x_ref[...] + mod_ref[2:3, :] * mix
    x1_ref[...] = x1
    h2 = (_rms(x1) * g2_ref[...]) * (1.0 + mod_ref[4:5, :]) + mod_ref[3:4, :]
    h2t = h2.T.astype(BF16)
    h2t_ref[...] = h2t
    qpt_ref[...] = jnp.dot(wqt_ref[...], h2t, preferred_element_type=F32).astype(BF16)


def _mix(x2, att2, cbh, conv_w, conv_g, w_o, mod3, g2, wq_t, S):
    T, D = x2.shape
    C = conv_w.shape[1]
    NQ = wq_t.shape[0]
    per_b = S // TOK_BLK
    rows8 = TOK_BLK // SUBLANES
    prev = lambda col: pl.BlockSpec((SUBLANES, C), lambda i: (jnp.maximum(i * rows8 - 1, 0), col))
    full = lambda a: pl.BlockSpec(a.shape, lambda i: (0,) * a.ndim)
    return pl.pallas_call(
        functools.partial(_mix_kernel, blocks_per_seq=per_b),
        grid=(T // TOK_BLK,),
        in_specs=[pl.BlockSpec((TOK_BLK, D), lambda i: (i, 0)),
                  pl.BlockSpec((TOK_BLK, ATT_WIDTH), lambda i: (i, 0)),
                  pl.BlockSpec((TOK_BLK, C), lambda i: (i, 0)),
                  pl.BlockSpec((TOK_BLK, C), lambda i: (i, 1)),
                  pl.BlockSpec((TOK_BLK, C), lambda i: (i, 2)),
                  prev(1), prev(2),
                  full(conv_w), full(conv_g), full(w_o),
                  pl.BlockSpec((None, 6, D), lambda i: (i // per_b, 0, 0)),
                  full(g2), full(wq_t)],
        out_specs=[pl.BlockSpec((TOK_BLK, D), lambda i: (i, 0)),
                   pl.BlockSpec((D, TOK_BLK), lambda i: (0, i)),
                   pl.BlockSpec((NQ, TOK_BLK), lambda i: (0, i))],
        out_shape=[jax.ShapeDtypeStruct((T, D), F32),
                   jax.ShapeDtypeStruct((D, T), BF16),
                   jax.ShapeDtypeStruct((NQ, T), BF16)],
        compiler_params=_cparams(("arbitrary",)),
        name="mix",
    )(x2, att2, cbh, cbh, cbh, cbh, cbh, conv_w, conv_g, w_o, mod3, g2, wq_t)


_PAIR_ROWS = ((0, 8, 0), (0, 8, 8), (1, 8, 0), (2, 5, 0), (3, 4, 0), (4, 3, 0),
              (5, 2, 0), (6, 2, 0), (7, 2, 0))


def _top_values(x, out_ref, rounds, with_rank=False):
    rank = jnp.full(x.shape, float(rounds), F32) if with_rank else None
    for r in range(rounds):
        m = jnp.max(x, axis=0, keepdims=True)
        out_ref[r:r + 1, :] = m
        taken = x == m
        if with_rank:
            rank = jnp.where(taken, float(r), rank)
        x = jnp.where(taken, -jnp.inf, x)
    return rank


def _select_kernel(qpt_ref, keys_ref, n1_ref, e1_ref, rf_ref, sc_ref, a_ref, b_ref, t_ref):
    TB = qpt_ref.shape[1]
    for hp in range(2 * PEER_HEADS):
        sc_ref[hp] = jnp.dot(keys_ref[hp], qpt_ref[hp * N_KEYS:(hp + 1) * N_KEYS, :],
                             preferred_element_type=F32)

    row8 = lax.broadcasted_iota(jnp.int32, (SUBLANES, LANES), 0)

    def per_head(h, carry):
        for lt in range(TB // LANES):
            lanes = slice(lt * LANES, (lt + 1) * LANES)
            s1 = sc_ref[2 * h, :, lanes]
            s2 = sc_ref[2 * h + 1, :, lanes]
            _top_values(s1, a_ref, PEER_TOPK)
            rank2 = _top_values(s2, b_ref, PEER_TOPK, with_rank=True)
            a = a_ref[...]
            b = b_ref[...]
            ea = jnp.exp(a - a[0:1, :])
            eb = jnp.exp(b - b[0:1, :])
            cands, prods = [], []
            for (i, n, j0) in _PAIR_ROWS:
                cs = a[i:i + 1, :] + b[j0:j0 + SUBLANES, :]
                ps = ea[i:i + 1, :] * eb[j0:j0 + SUBLANES, :]
                if n < SUBLANES:
                    cs = jnp.where(row8 < n, cs, -jnp.inf)
                cands.append(cs)
                prods.append(ps)
            cands.append(a[SUBLANES:, :] + b[0:1, :])
            prods.append(ea[SUBLANES:, :] * eb[0:1, :])
            cand = jnp.concatenate(cands, axis=0)
            prod = jnp.concatenate(prods, axis=0)
            _top_values(cand, t_ref, PEER_TOPK)
            tau = t_ref[PEER_TOPK - 1:PEER_TOPK, :]
            z = jnp.sum(jnp.where(cand >= tau, prod, 0.0), axis=0, keepdims=True)
            n1 = jnp.zeros_like(s1)
            for j in range(PEER_TOPK):
                n1 = n1 + jnp.where(s1 + b[j:j + 1, :] >= tau, 1.0, 0.0)
            n1_ref[h, :, lanes] = n1
            e1_ref[h, :, lanes] = 0.5 * jnp.exp(s1 - a[0:1, :])
            r2 = pltpu.bitcast(rank2.astype(BF16), jnp.int32)
            f2 = pltpu.bitcast((jnp.exp(s2 - b[0:1, :]) / z).astype(BF16), jnp.int32)
            for k in range(N_KEYS // BF16_ROWS):
                ks = slice(k * SUBLANES, (k + 1) * SUBLANES)
                rf_ref[h, lt, 2 * k] = r2[ks, :]
                rf_ref[h, lt, 2 * k + 1] = f2[ks, :]
        return carry

    lax.fori_loop(0, PEER_HEADS, per_head, 0)


def _select(qp_t, keys):
    NQ, T = qp_t.shape
    H = PEER_HEADS
    rows = pl.BlockSpec((H, N_KEYS, TOK_BLK), lambda i: (0, 0, i))
    rows_shape = jax.ShapeDtypeStruct((H, N_KEYS, T), F32)
    n_vregs = 2 * (N_KEYS // BF16_ROWS)
    tiles = pl.BlockSpec((H, TOK_BLK // LANES, n_vregs, SUBLANES, LANES), lambda i: (0, i, 0, 0, 0))
    tiles_shape = jax.ShapeDtypeStruct((H, T // LANES, n_vregs, SUBLANES, LANES), jnp.int32)
    return pl.pallas_call(
        _select_kernel,
        grid=(T // TOK_BLK,),
        in_specs=[pl.BlockSpec((NQ, TOK_BLK), lambda i: (0, i)),
                  pl.BlockSpec(keys.shape, lambda i: (0, 0, 0))],
        out_specs=[rows, rows, tiles],
        out_shape=[rows_shape, rows_shape, tiles_shape],
        scratch_shapes=[pltpu.VMEM((2 * H, N_KEYS, TOK_BLK), F32),
                        pltpu.VMEM((PEER_TOPK, LANES), F32),
                        pltpu.VMEM((PEER_TOPK, LANES), F32),
                        pltpu.VMEM((PEER_TOPK, LANES), F32)],
        compiler_params=_cparams(("arbitrary",)),
        name="select",
    )(qp_t, keys)


def _expert_kernel(h2t_ref, u_ref, vt_ref, n1_ref, e1_ref, rf_ref,
                   x1_ref, mod_ref, fg_ref, o_ref, acc_ref, a_ref, p_ref):
    c = pl.program_id(1)
    EC, TB = a_ref.shape

    @pl.when(c == 0)
    def _():
        acc_ref[...] = jnp.zeros_like(acc_ref)

    a_ref[...] = jnp.dot(u_ref[...], h2t_ref[...], preferred_element_type=F32)

    def per_first_key(j, carry):
        base = pl.multiple_of(j * N_KEYS, N_KEYS)
        for lt in range(TB // LANES):
            lanes = slice(lt * LANES, (lt + 1) * LANES)
            w = [jnp.zeros((BF16_ROWS, LANES), BF16) for _ in range(N_KEYS // BF16_ROWS)]
            for h in range(PEER_HEADS):
                n1 = jnp.broadcast_to(n1_ref[h, j, :, lanes], (BF16_ROWS, LANES)).astype(BF16)
                e1 = jnp.broadcast_to(e1_ref[h, j, :, lanes], (BF16_ROWS, LANES)).astype(BF16)
                for k in range(len(w)):
                    hit = pltpu.bitcast(rf_ref[h, lt, 2 * k], BF16) < n1
                    f2 = pltpu.bitcast(rf_ref[h, lt, 2 * k + 1], BF16)
                    w[k] = w[k] + jnp.where(hit, f2 * e1, jnp.zeros_like(e1))
            for k in range(len(w)):
                rows = pl.ds(base + k * BF16_ROWS, BF16_ROWS)
                a = a_ref[rows, lanes]
                gelu2 = a * (1.0 + lax.erf(a * INV_SQRT2))
                p_ref[rows, lanes] = w[k] * gelu2.astype(BF16)
        return carry

    lax.fori_loop(0, EC // N_KEYS, per_first_key, 0)
    acc_ref[...] += jnp.dot(vt_ref[...], p_ref[...], preferred_element_type=F32)

    @pl.when(c == pl.num_programs(1) - 1)
    def _():
        x2 = x1_ref[...] + mod_ref[5:6, :] * acc_ref[...].T
        o_ref[...] = _rms(x2) * fg_ref[...]


def _experts(h2_t, u_b, v_t, n1, e1, rf, x1, mod3, final_g, S):
    D, T = h2_t.shape
    E = u_b.shape[0]
    H = PEER_HEADS
    TB, EC = TOK_BLK, EXPERT_BLK
    per_b = S // TB
    n1 = n1.reshape(H, N_KEYS, 1, T)
    e1 = e1.reshape(H, N_KEYS, 1, T)
    rows = pl.BlockSpec((H, EC // N_KEYS, 1, TB), lambda i, c: (0, c, 0, i))
    tiles = pl.BlockSpec((H, TB // LANES) + rf.shape[2:], lambda i, c: (0, i, 0, 0, 0))
    return pl.pallas_call(
        _expert_kernel,
        grid=(T // TB, E // EC),
        in_specs=[pl.BlockSpec((D, TB), lambda i, c: (0, i)),
                  pl.BlockSpec((EC, D), lambda i, c: (c, 0)),
                  pl.BlockSpec((D, EC), lambda i, c: (0, c)),
                  rows, rows, tiles,
                  pl.BlockSpec((TB, D), lambda i, c: (i, 0)),
                  pl.BlockSpec((None, 6, D), lambda i, c: (i // per_b, 0, 0)),
                  pl.BlockSpec((1, D), lambda i, c: (0, 0))],
        out_specs=pl.BlockSpec((TB, D), lambda i, c: (i, 0)),
        out_shape=jax.ShapeDtypeStruct((T, D), F32),
        scratch_shapes=[pltpu.VMEM((D, TB), F32),
                        pltpu.VMEM((EC, TB), F32),
                        pltpu.VMEM((EC, TB), BF16)],
        compiler_params=_cparams(("arbitrary", "arbitrary")),
        name="experts",
    )(h2_t, u_b, v_t, n1, e1, rf, x1, mod3, final_g)


def kernel(x, c, ada_w, ada_b, norm1_g, w_in, conv_w, lam_q1, lam_k1, lam_q2, lam_k2, attn_norm_g, conv_norm_g, w_o, norm2_g, peer_wq, peer_keys, peer_u, peer_v, final_g):
    B, S, D = x.shape
    T = B * S
    assert ada_w.shape[0] == 1, "single-layer block"
    assert S % TOK_BLK == 0 and S % Q_BLK == 0 and Q_BLK % CHUNK == 0
    assert peer_u.shape[1] % EXPERT_BLK == 0 and (EXPERT_BLK // N_KEYS) % SUBLANES == 0

    x2 = x.reshape(T, D)
    mod3 = _modulation(c, ada_w[0], ada_b).reshape(B, 6, D)

    qkv, cbh = _inproj(x2, mod3, norm1_g, w_in[0].astype(BF16), S)

    lam_vecs = jnp.concatenate([lam_q1, lam_k1, lam_q2, lam_k2], axis=0)
    att = _attention(qkv.reshape(B, S, -1), lam_vecs, attn_norm_g)

    x1, h2_t, qp_t = _mix(x2, att.reshape(T, ATT_WIDTH), cbh, conv_w[0], conv_norm_g,
                          w_o[0].astype(BF16), mod3, norm2_g, peer_wq[0].T.astype(BF16), S)

    keys = peer_keys[0].reshape(2 * PEER_HEADS, N_KEYS, -1).astype(BF16)
    n1, e1, rf = _select(qp_t, keys)

    out = _experts(h2_t, peer_u[0].astype(BF16), peer_v[0].T.astype(BF16),
                   n1, e1, rf, x1, mod3, final_g.reshape(1, D), S)
    return out.reshape(B, S, D)
```

```python
import functools
import math

import jax
import jax.numpy as jnp
from jax import lax
from jax.experimental import pallas as pl
from jax.experimental.pallas import tpu as pltpu

F32 = jnp.float32
BF16 = jnp.bfloat16

EPS = 1e-6
CHUNK = 64
ATT_HEADS = 4
ATT_HEAD_DIM = 64
ATT_V_DIM = 128
ATT_WIDTH = ATT_HEADS * ATT_V_DIM
CONV_K = 3
PEER_HEADS = 8
N_KEYS = 128
PEER_TOPK = 16
LAM_INIT = 0.8 - 0.6 * math.exp(0.0)
INV_SQRT2 = 0.7071067811865476

LANES = 128
SUBLANES = 8
BF16_ROWS = 2 * SUBLANES
Q_BLK = 256
TOK_BLK = 512
EXPERT_BLK = 2048
EXPERT_GRP = 256
VMEM_LIMIT = 48 * 1024 * 1024


def _cparams(sem):
    return pltpu.CompilerParams(dimension_semantics=sem, vmem_limit_bytes=VMEM_LIMIT)


def _rms(x):
    return x * lax.rsqrt(jnp.mean(x * x, axis=-1, keepdims=True) + EPS)


def _mod_kernel(c_ref, w_ref, b_ref, o_ref):
    sc = jax.nn.silu(c_ref[...])
    o_ref[...] = jnp.dot(sc.astype(BF16), w_ref[...].astype(BF16),
                         preferred_element_type=F32) + b_ref[...]


def _modulation(c, ada_w, ada_b):
    B, D = c.shape
    N = ada_w.shape[1]
    return pl.pallas_call(
        _mod_kernel,
        grid=(N // D,),
        in_specs=[pl.BlockSpec((B, D), lambda j: (0, 0)),
                  pl.BlockSpec((D, D), lambda j: (0, j)),
                  pl.BlockSpec((1, D), lambda j: (0, j))],
        out_specs=pl.BlockSpec((B, D), lambda j: (0, j)),
        out_shape=jax.ShapeDtypeStruct((B, N), F32),
        compiler_params=_cparams(("arbitrary",)),
        name="mod",
    )(c, ada_w, ada_b)


def _inproj_kernel(x_ref, mod_ref, g_ref, w_ref, qkv_ref, cbh_ref):
    y = _rms(x_ref[...])
    h = (y * g_ref[...]) * (1.0 + mod_ref[1:2, :]) + mod_ref[0:1, :]
    p = jnp.dot(h.astype(BF16), w_ref[...], preferred_element_type=F32)
    n_att = qkv_ref.shape[1]
    qkv_ref[...] = p[:, :n_att].astype(BF16)
    cbh_ref[...] = p[:, n_att:]


def _inproj(x2, mod3, g, w, S):
    T, D = x2.shape
    n_att = 3 * ATT_WIDTH
    n_conv = w.shape[1] - n_att
    per_b = S // TOK_BLK
    return pl.pallas_call(
        _inproj_kernel,
        grid=(T // TOK_BLK,),
        in_specs=[pl.BlockSpec((TOK_BLK, D), lambda i: (i, 0)),
                  pl.BlockSpec((None, 6, D), lambda i: (i // per_b, 0, 0)),
                  pl.BlockSpec((1, D), lambda i: (0, 0)),
                  pl.BlockSpec(w.shape, lambda i: (0, 0))],
        out_specs=[pl.BlockSpec((TOK_BLK, n_att), lambda i: (i, 0)),
                   pl.BlockSpec((TOK_BLK, n_conv), lambda i: (i, 0))],
        out_shape=[jax.ShapeDtypeStruct((T, n_att), BF16),
                   jax.ShapeDtypeStruct((T, n_conv), F32)],
        compiler_params=_cparams(("arbitrary",)),
        name="inproj",
    )(x2, mod3, g, w)


def _attn_kernel(q_ref, k_ref, v_ref, lam_ref, g_ref, o_ref, bias_ref):
    S = q_ref.shape[0]
    h = pl.program_id(1)

    @pl.when(pl.program_id(0) == 0)
    def _():
        slope = jnp.where(h == 0, 2.0 ** -2,
                          jnp.where(h == 1, 2.0 ** -4,
                                    jnp.where(h == 2, 2.0 ** -6, 2.0 ** -8))).astype(F32)
        r = lax.broadcasted_iota(jnp.int32, (Q_BLK, S), 0) + (S - Q_BLK)
        x = lax.broadcasted_iota(jnp.int32, (Q_BLK, S), 1)
        bias = -slope * jnp.abs(r - x).astype(F32)
        allowed = (x // CHUNK) <= (r // CHUNK)
        bias_ref[h] = jnp.where(allowed, bias, -jnp.inf)

    lv = lam_ref[...]
    lam = (jnp.exp(jnp.sum(lv[0:1, :] * lv[1:2, :], axis=-1, keepdims=True))
           - jnp.exp(jnp.sum(lv[2:3, :] * lv[3:4, :], axis=-1, keepdims=True))
           + LAM_INIT)
    gain = g_ref[...] * (1.0 - LAM_INIT)
    nt = (((1,), (1,)), ((), ()))
    for j in range(S // Q_BLK):
        q0 = j * Q_BLK
        L = q0 + Q_BLK
        qb = q_ref[q0:q0 + Q_BLK, :]
        lane = lax.broadcasted_iota(jnp.int32, qb.shape, 1)
        scale = jnp.asarray(ATT_HEAD_DIM ** -0.5, BF16)
        zero = jnp.zeros_like(qb)
        q_lo = jnp.where(lane < ATT_HEAD_DIM, qb, zero) * scale
        q_hi = jnp.where(lane >= ATT_HEAD_DIM, qb, zero) * scale
        kk = k_ref[0:L, :]
        vv = v_ref[0:L, :]

        def half(qh):
            s = lax.dot_general(qh, kk, nt, preferred_element_type=F32) + bias_ref[h, :, S - L:S]
            m = jnp.max(s, axis=-1, keepdims=True)
            p = jnp.exp(s - m)
            l = jnp.sum(p, axis=-1, keepdims=True)
            o = jnp.dot(p.astype(BF16), vv, preferred_element_type=F32)
            return o / l

        o = half(q_lo) - lam * half(q_hi)
        o_ref[q0:q0 + Q_BLK, :] = (_rms(o) * gain).astype(o_ref.dtype)


def _attention(qkv3, lam_vecs, attn_g):
    B, S, _ = qkv3.shape
    dv = ATT_V_DIM
    blk = lambda off: pl.BlockSpec((None, S, dv), lambda b, h: (b, 0, off + h))
    return pl.pallas_call(
        _attn_kernel,
        grid=(B, ATT_HEADS),
        in_specs=[blk(0), blk(ATT_HEADS), blk(2 * ATT_HEADS),
                  pl.BlockSpec(lam_vecs.shape, lambda b, h: (0, 0)),
                  pl.BlockSpec((1, dv), lambda b, h: (0, h))],
        out_specs=pl.BlockSpec((None, S, dv), lambda b, h: (b, 0, h)),
        out_shape=jax.ShapeDtypeStruct((B, S, ATT_WIDTH), BF16),
        scratch_shapes=[pltpu.VMEM((ATT_HEADS, Q_BLK, S), F32)],
        compiler_params=_cparams(("arbitrary", "arbitrary")),
        name="attn",
    )(qkv3, qkv3, qkv3, lam_vecs, attn_g)


def _mix_kernel(x_ref, att_ref, cb_ref, cc_ref, ch_ref, ccp_ref, chp_ref, cw_ref, cg_ref,
                wo_ref, mod_ref, g2_ref, wqt_ref, x1_ref, h2t_ref, qpt_ref, *, blocks_per_seq):
    i = pl.program_id(0)
    u = cc_ref[...] * ch_ref[...]
    first = (i % blocks_per_seq) == 0
    up = jnp.where(first, 0.0, ccp_ref[...] * chp_ref[...])
    row = lax.broadcasted_iota(jnp.int32, up.shape, 0)

    def shifted(k):
        uk = pltpu.roll(u, k, 0)
        top = jnp.where(row < k, pltpu.roll(up, k, 0), uk[0:SUBLANES, :])
        return jnp.concatenate([top, uk[SUBLANES:, :]], axis=0)

    cw = cw_ref[...]
    y = cw[2:3, :] * u + cw[1:2, :] * shifted(1) + cw[0:1, :] * shifted(2)
    cv = _rms(cb_ref[...] * y) * cg_ref[...]
    both = jnp.concatenate([att_ref[...], cv.astype(BF16)], axis=-1)
    mix = jnp.dot(both, wo_ref[...], preferred_element_type=F32)
    x1 = x_ref[...] + mod_ref[2:3, :] * mix
    x1_ref[...] = x1
    h2 = (_rms(x1) * g2_ref[...]) * (1.0 + mod_ref[4:5, :]) + mod_ref[3:4, :]
    h2t = h2.T.astype(BF16)
    h2t_ref[...] = h2t
    qpt_ref[...] = jnp.dot(wqt_ref[...], h2t, preferred_element_type=F32).astype(BF16)


def _mix(x2, att2, cbh, conv_w, conv_g, w_o, mod3, g2, wq_t, S):
    T, D = x2.shape
    C = conv_w.shape[1]
    NQ = wq_t.shape[0]
    per_b = S // TOK_BLK
    rows8 = TOK_BLK // SUBLANES
    prev = lambda col: pl.BlockSpec((SUBLANES, C), lambda i: (jnp.maximum(i * rows8 - 1, 0), col))
    full = lambda a: pl.BlockSpec(a.shape, lambda i: (0,) * a.ndim)
    return pl.pallas_call(
        functools.partial(_mix_kernel, blocks_per_seq=per_b),
        grid=(T // TOK_BLK,),
        in_specs=[pl.BlockSpec((TOK_BLK, D), lambda i: (i, 0)),
                  pl.BlockSpec((TOK_BLK, ATT_WIDTH), lambda i: (i, 0)),
                  pl.BlockSpec((TOK_BLK, C), lambda i: (i, 0)),
                  pl.BlockSpec((TOK_BLK, C), lambda i: (i, 1)),
                  pl.BlockSpec((TOK_BLK, C), lambda i: (i, 2)),
                  prev(1), prev(2),
                  full(conv_w), full(conv_g), full(w_o),
                  pl.BlockSpec((None, 6, D), lambda i: (i // per_b, 0, 0)),
                  full(g2), full(wq_t)],
        out_specs=[pl.BlockSpec((TOK_BLK, D), lambda i: (i, 0)),
                   pl.BlockSpec((D, TOK_BLK), lambda i: (0, i)),
                   pl.BlockSpec((NQ, TOK_BLK), lambda i: (0, i))],
        out_shape=[jax.ShapeDtypeStruct((T, D), F32),
                   jax.ShapeDtypeStruct((D, T), BF16),
                   jax.ShapeDtypeStruct((NQ, T), BF16)],
        compiler_params=_cparams(("arbitrary",)),
        name="mix",
    )(x2, att2, cbh, cbh, cbh, cbh, cbh, conv_w, conv_g, w_o, mod3, g2, wq_t)


_PAIR_ROWS = ((0, 8, 0), (0, 8, 8), (1, 8, 0), (2, 5, 0), (3, 4, 0), (4, 3, 0),
              (5, 2, 0), (6, 2, 0), (7, 2, 0))


def _top_values(x, out_ref, rounds, with_rank=False):
    rank = jnp.full(x.shape, float(rounds), F32) if with_rank else None
    for r in range(rounds):
        m = jnp.max(x, axis=0, keepdims=True)
        out_ref[r:r + 1, :] = m
        taken = x == m
        if with_rank:
            rank = jnp.where(taken, float(r), rank)
        x = jnp.where(taken, -jnp.inf, x)
    return rank


def _select_kernel(qpt_ref, keys_ref, n1_ref, e1_ref, rf_ref, sc_ref, a_ref, b_ref, t_ref):
    TB = qpt_ref.shape[1]
    for hp in range(2 * PEER_HEADS):
        sc_ref[hp] = jnp.dot(keys_ref[hp], qpt_ref[hp * N_KEYS:(hp + 1) * N_KEYS, :],
                             preferred_element_type=F32)

    row8 = lax.broadcasted_iota(jnp.int32, (SUBLANES, LANES), 0)
    row16 = lax.broadcasted_iota(jnp.int32, (PEER_TOPK, LANES), 0)

    def per_head(h, carry):
        for lt in range(TB // LANES):
            lanes = slice(lt * LANES, (lt + 1) * LANES)
            s1 = sc_ref[2 * h, :, lanes]
            s2 = sc_ref[2 * h + 1, :, lanes]
            _top_values(s1, a_ref, PEER_TOPK)
            rank2 = _top_values(s2, b_ref, PEER_TOPK, with_rank=True)
            a = a_ref[...]
            b = b_ref[...]
            ea = jnp.exp(a - a[0:1, :])
            eb = jnp.exp(b - b[0:1, :])
            cands, prods = [], []
            for (i, n, j0) in _PAIR_ROWS:
                cs = a[i:i + 1, :] + b[j0:j0 + SUBLANES, :]
                ps = ea[i:i + 1, :] * eb[j0:j0 + SUBLANES, :]
                if n < SUBLANES:
                    cs = jnp.where(row8 < n, cs, -jnp.inf)
                cands.append(cs)
                prods.append(ps)
            cands.append(a[SUBLANES:, :] + b[0:1, :])
            prods.append(ea[SUBLANES:, :] * eb[0:1, :])
            cand = jnp.concatenate(cands, axis=0)
            prod = jnp.concatenate(prods, axis=0)
            _top_values(cand, t_ref, PEER_TOPK)
            tau = t_ref[PEER_TOPK - 1:PEER_TOPK, :]
            z = jnp.sum(jnp.where(cand >= tau, prod, 0.0), axis=0, keepdims=True)
            n1 = jnp.zeros_like(s1)
            for j in range(4):
                n1 = n1 + jnp.where(s1 + b[j:j + 1, :] >= tau, 1.0, 0.0)
            extra = jnp.zeros_like(s1)
            for k in reversed(range(3)):
                beyond = jnp.where((a[k:k + 1, :] + b >= tau) & (row16 >= 4), 1.0, 0.0)
                extra = jnp.where(s1 == a[k:k + 1, :], jnp.sum(beyond, axis=0, keepdims=True), extra)
            n1 = n1 + extra
            n1_ref[h, :, lanes] = n1
            e1_ref[h, :, lanes] = 0.5 * jnp.exp(s1 - a[0:1, :])
            r2 = pltpu.bitcast(rank2.astype(BF16), jnp.int32)
            f2 = pltpu.bitcast((jnp.exp(s2 - b[0:1, :]) / z).astype(BF16), jnp.int32)
            for k in range(N_KEYS // BF16_ROWS):
                ks = slice(k * SUBLANES, (k + 1) * SUBLANES)
                rf_ref[h, lt, 2 * k] = r2[ks, :]
                rf_ref[h, lt, 2 * k + 1] = f2[ks, :]
        return carry

    lax.fori_loop(0, PEER_HEADS, per_head, 0)


def _select(qp_t, keys):
    NQ, T = qp_t.shape
    H = PEER_HEADS
    rows = pl.BlockSpec((H, N_KEYS, TOK_BLK), lambda i: (0, 0, i))
    rows_shape = jax.ShapeDtypeStruct((H, N_KEYS, T), F32)
    n_vregs = 2 * (N_KEYS // BF16_ROWS)
    tiles = pl.BlockSpec((H, TOK_BLK // LANES, n_vregs, SUBLANES, LANES), lambda i: (0, i, 0, 0, 0))
    tiles_shape = jax.ShapeDtypeStruct((H, T // LANES, n_vregs, SUBLANES, LANES), jnp.int32)
    return pl.pallas_call(
        _select_kernel,
        grid=(T // TOK_BLK,),
        in_specs=[pl.BlockSpec((NQ, TOK_BLK), lambda i: (0, i)),
                  pl.BlockSpec(keys.shape, lambda i: (0, 0, 0))],
        out_specs=[rows, rows, tiles],
        out_shape=[rows_shape, rows_shape, tiles_shape],
        scratch_shapes=[pltpu.VMEM((2 * H, N_KEYS, TOK_BLK), F32),
                        pltpu.VMEM((PEER_TOPK, LANES), F32),
                        pltpu.VMEM((PEER_TOPK, LANES), F32),
                        pltpu.VMEM((PEER_TOPK, LANES), F32)],
        compiler_params=_cparams(("arbitrary",)),
        name="select",
    )(qp_t, keys)


def _expert_kernel(h2t_ref, u_ref, vt_ref, n1_ref, e1_ref, rf_ref,
                   x1_ref, mod_ref, fg_ref, o_ref, acc_ref, a_ref, p_ref):
    c = pl.program_id(1)
    EC, TB = a_ref.shape

    @pl.when(c == 0)
    def _():
        acc_ref[...] = jnp.zeros_like(acc_ref)

    groups = [slice(g * EXPERT_GRP, (g + 1) * EXPERT_GRP) for g in range(EC // EXPERT_GRP)]
    for grp in groups:
        a_ref[grp, :] = jnp.dot(u_ref[grp, :], h2t_ref[...], preferred_element_type=F32)

    for j in range(EC // N_KEYS):
        for lt in range(TB // LANES):
            lanes = slice(lt * LANES, (lt + 1) * LANES)
            w = [jnp.zeros((BF16_ROWS, LANES), BF16) for _ in range(N_KEYS // BF16_ROWS)]
            for h in range(PEER_HEADS):
                n1 = jnp.broadcast_to(n1_ref[h, j:j + 1, lanes], (BF16_ROWS, LANES)).astype(BF16)
                e1 = jnp.broadcast_to(e1_ref[h, j:j + 1, lanes], (BF16_ROWS, LANES)).astype(BF16)
                for k in range(len(w)):
                    hit = pltpu.bitcast(rf_ref[h, lt, 2 * k], BF16) < n1
                    f2 = pltpu.bitcast(rf_ref[h, lt, 2 * k + 1], BF16)
                    w[k] = w[k] + jnp.where(hit, f2 * e1, jnp.zeros_like(e1))
            for k in range(len(w)):
                rows = slice(j * N_KEYS + k * BF16_ROWS, j * N_KEYS + (k + 1) * BF16_ROWS)
                a = a_ref[rows, lanes]
                gelu2 = a * (1.0 + lax.erf(a * INV_SQRT2))
                p_ref[rows, lanes] = w[k] * gelu2.astype(BF16)

    update = None
    for grp in groups:
        part = jnp.dot(vt_ref[:, grp], p_ref[grp, :], preferred_element_type=F32)
        update = part if update is None else update + part
    acc_ref[...] += update

    @pl.when(c == pl.num_programs(1) - 1)
    def _():
        x2 = x1_ref[...] + mod_ref[5:6, :] * acc_ref[...].T
        o_ref[...] = _rms(x2) * fg_ref[...]


def _experts(h2_t, u_b, v_t, n1, e1, rf, x1, mod3, final_g, S):
    D, T = h2_t.shape
    E = u_b.shape[0]
    H = PEER_HEADS
    TB, EC = TOK_BLK, EXPERT_BLK
    per_b = S // TB
    rows = pl.BlockSpec((H, EC // N_KEYS, TB), lambda i, c: (0, c, i))
    tiles = pl.BlockSpec((H, TB // LANES) + rf.shape[2:], lambda i, c: (0, i, 0, 0, 0))
    return pl.pallas_call(
        _expert_kernel,
        grid=(T // TB, E // EC),
        in_specs=[pl.BlockSpec((D, TB), lambda i, c: (0, i)),
                  pl.BlockSpec((EC, D), lambda i, c: (c, 0)),
                  pl.BlockSpec((D, EC), lambda i, c: (0, c)),
                  rows, rows, tiles,
                  pl.BlockSpec((TB, D), lambda i, c: (i, 0)),
                  pl.BlockSpec((None, 6, D), lambda i, c: (i // per_b, 0, 0)),
                  pl.BlockSpec((1, D), lambda i, c: (0, 0))],
        out_specs=pl.BlockSpec((TB, D), lambda i, c: (i, 0)),
        out_shape=jax.ShapeDtypeStruct((T, D), F32),
        scratch_shapes=[pltpu.VMEM((D, TB), F32),
                        pltpu.VMEM((EC, TB), F32),
                        pltpu.VMEM((EC, TB), BF16)],
        compiler_params=_cparams(("arbitrary", "arbitrary")),
        name="experts",
    )(h2_t, u_b, v_t, n1, e1, rf, x1, mod3, final_g)


def kernel(x, c, ada_w, ada_b, norm1_g, w_in, conv_w, lam_q1, lam_k1, lam_q2, lam_k2, attn_norm_g, conv_norm_g, w_o, norm2_g, peer_wq, peer_keys, peer_u, peer_v, final_g):
    B, S, D = x.shape
    T = B * S
    assert ada_w.shape[0] == 1, "single-layer block"
    assert S % TOK_BLK == 0 and S % Q_BLK == 0 and Q_BLK % CHUNK == 0
    assert peer_u.shape[1] % EXPERT_BLK == 0 and (EXPERT_BLK // N_KEYS) % SUBLANES == 0

    x2 = x.reshape(T, D)
    mod3 = _modulation(c, ada_w[0], ada_b).reshape(B, 6, D)

    qkv, cbh = _inproj(x2, mod3, norm1_g, w_in[0].astype(BF16), S)

    lam_vecs = jnp.concatenate([lam_q1, lam_k1, lam_q2, lam_k2], axis=0)
    att = _attention(qkv.reshape(B, S, -1), lam_vecs, attn_norm_g)

    x1, h2_t, qp_t = _mix(x2, att.reshape(T, ATT_WIDTH), cbh, conv_w[0], conv_norm_g,
                          w_o[0].astype(BF16), mod3, norm2_g, peer_wq[0].T.astype(BF16), S)

    keys = peer_keys[0].reshape(2 * PEER_HEADS, N_KEYS, -1).astype(BF16)
    n1, e1, rf = _select(qp_t, keys)

    out = _experts(h2_t, peer_u[0].astype(BF16), peer_v[0].T.astype(BF16),
                   n1, e1, rf, x1, mod3, final_g.reshape(1, D), S)
    return out.reshape(B, S, D)
```

```python
import functools
import math

import jax
import jax.numpy as jnp
from jax import lax
from jax.experimental import pallas as pl
from jax.experimental.pallas import tpu as pltpu

F32 = jnp.float32
BF16 = jnp.bfloat16

EPS = 1e-6
CHUNK = 64
ATT_HEADS = 4
ATT_HEAD_DIM = 64
ATT_V_DIM = 128
ATT_WIDTH = ATT_HEADS * ATT_V_DIM
CONV_K = 3
PEER_HEADS = 8
N_KEYS = 128
PEER_TOPK = 16
LAM_INIT = 0.8 - 0.6 * math.exp(0.0)
INV_SQRT2 = 0.7071067811865476

LANES = 128
SUBLANES = 8
BF16_ROWS = 2 * SUBLANES
Q_BLK = 256
TOK_BLK = 512
EXPERT_BLK = 2048
EXPERT_GRP = 256
VMEM_LIMIT = 48 * 1024 * 1024


def _cparams(sem):
    return pltpu.CompilerParams(dimension_semantics=sem, vmem_limit_bytes=VMEM_LIMIT)


def _rms(x):
    return x * lax.rsqrt(jnp.mean(x * x, axis=-1, keepdims=True) + EPS)


def _mod_kernel(c_ref, w_ref, b_ref, o_ref):
    sc = jax.nn.silu(c_ref[...])
    o_ref[...] = jnp.dot(sc.astype(BF16), w_ref[...].astype(BF16),
                         preferred_element_type=F32) + b_ref[...]


def _modulation(c, ada_w, ada_b):
    B, D = c.shape
    N = ada_w.shape[1]
    return pl.pallas_call(
        _mod_kernel,
        grid=(N // D,),
        in_specs=[pl.BlockSpec((B, D), lambda j: (0, 0)),
                  pl.BlockSpec((D, D), lambda j: (0, j)),
                  pl.BlockSpec((1, D), lambda j: (0, j))],
        out_specs=pl.BlockSpec((B, D), lambda j: (0, j)),
        out_shape=jax.ShapeDtypeStruct((B, N), F32),
        compiler_params=_cparams(("arbitrary",)),
        name="mod",
    )(c, ada_w, ada_b)


def _inproj_kernel(x_ref, mod_ref, g_ref, w_ref, qkv_ref, cbh_ref):
    y = _rms(x_ref[...])
    h = (y * g_ref[...]) * (1.0 + mod_ref[1:2, :]) + mod_ref[0:1, :]
    p = jnp.dot(h.astype(BF16), w_ref[...], preferred_element_type=F32)
    n_att = qkv_ref.shape[1]
    qkv_ref[...] = p[:, :n_att].astype(BF16)
    cbh_ref[...] = p[:, n_att:]


def _inproj(x2, mod3, g, w, S):
    T, D = x2.shape
    n_att = 3 * ATT_WIDTH
    n_conv = w.shape[1] - n_att
    per_b = S // TOK_BLK
    return pl.pallas_call(
        _inproj_kernel,
        grid=(T // TOK_BLK,),
        in_specs=[pl.BlockSpec((TOK_BLK, D), lambda i: (i, 0)),
                  pl.BlockSpec((None, 6, D), lambda i: (i // per_b, 0, 0)),
                  pl.BlockSpec((1, D), lambda i: (0, 0)),
                  pl.BlockSpec(w.shape, lambda i: (0, 0))],
        out_specs=[pl.BlockSpec((TOK_BLK, n_att), lambda i: (i, 0)),
                   pl.BlockSpec((TOK_BLK, n_conv), lambda i: (i, 0))],
        out_shape=[jax.ShapeDtypeStruct((T, n_att), BF16),
                   jax.ShapeDtypeStruct((T, n_conv), F32)],
        compiler_params=_cparams(("arbitrary",)),
        name="inproj",
    )(x2, mod3, g, w)


def _attn_kernel(q_ref, k_ref, v_ref, lam_ref, g_ref, o_ref, bias_ref):
    S = q_ref.shape[0]
    h = pl.program_id(1)

    @pl.when(pl.program_id(0) == 0)
    def _():
        slope = jnp.where(h == 0, 2.0 ** -2,
                          jnp.where(h == 1, 2.0 ** -4,
                                    jnp.where(h == 2, 2.0 ** -6, 2.0 ** -8))).astype(F32)
        r = lax.broadcasted_iota(jnp.int32, (Q_BLK, S), 0) + (S - Q_BLK)
        x = lax.broadcasted_iota(jnp.int32, (Q_BLK, S), 1)
        bias = -slope * jnp.abs(r - x).astype(F32)
        allowed = (x // CHUNK) <= (r // CHUNK)
        bias_ref[h] = jnp.where(allowed, bias, -jnp.inf)

    lv = lam_ref[...]
    lam = (jnp.exp(jnp.sum(lv[0:1, :] * lv[1:2, :], axis=-1, keepdims=True))
           - jnp.exp(jnp.sum(lv[2:3, :] * lv[3:4, :], axis=-1, keepdims=True))
           + LAM_INIT)
    gain = g_ref[...] * (1.0 - LAM_INIT)
    nt = (((1,), (1,)), ((), ()))
    for j in range(S // Q_BLK):
        q0 = j * Q_BLK
        L = q0 + Q_BLK
        qb = q_ref[q0:q0 + Q_BLK, :]
        lane = lax.broadcasted_iota(jnp.int32, qb.shape, 1)
        scale = jnp.asarray(ATT_HEAD_DIM ** -0.5, BF16)
        zero = jnp.zeros_like(qb)
        q_lo = jnp.where(lane < ATT_HEAD_DIM, qb, zero) * scale
        q_hi = jnp.where(lane >= ATT_HEAD_DIM, qb, zero) * scale
        kk = k_ref[0:L, :]
        vv = v_ref[0:L, :]

        def half(qh):
            s = lax.dot_general(qh, kk, nt, preferred_element_type=F32) + bias_ref[h, :, S - L:S]
            m = jnp.max(s, axis=-1, keepdims=True)
            p = jnp.exp(s - m)
            l = jnp.sum(p, axis=-1, keepdims=True)
            o = jnp.dot(p.astype(BF16), vv, preferred_element_type=F32)
            return o / l

        o = half(q_lo) - lam * half(q_hi)
        o_ref[q0:q0 + Q_BLK, :] = (_rms(o) * gain).astype(o_ref.dtype)


def _attention(qkv3, lam_vecs, attn_g):
    B, S, _ = qkv3.shape
    dv = ATT_V_DIM
    blk = lambda off: pl.BlockSpec((None, S, dv), lambda b, h: (b, 0, off + h))
    return pl.pallas_call(
        _attn_kernel,
        grid=(B, ATT_HEADS),
        in_specs=[blk(0), blk(ATT_HEADS), blk(2 * ATT_HEADS),
                  pl.BlockSpec(lam_vecs.shape, lambda b, h: (0, 0)),
                  pl.BlockSpec((1, dv), lambda b, h: (0, h))],
        out_specs=pl.BlockSpec((None, S, dv), lambda b, h: (b, 0, h)),
        out_shape=jax.ShapeDtypeStruct((B, S, ATT_WIDTH), BF16),
        scratch_shapes=[pltpu.VMEM((ATT_HEADS, Q_BLK, S), F32)],
        compiler_params=_cparams(("arbitrary", "arbitrary")),
        name="attn",
    )(qkv3, qkv3, qkv3, lam_vecs, attn_g)


def _mix_kernel(x_ref, att_ref, cb_ref, cc_ref, ch_ref, ccp_ref, chp_ref, cw_ref, cg_ref,
                wo_ref, mod_ref, g2_ref, wqt_ref, x1_ref, h2t_ref, qpt_ref, *, blocks_per_seq):
    i = pl.program_id(0)
    u = cc_ref[...] * ch_ref[...]
    first = (i % blocks_per_seq) == 0
    up = jnp.where(first, 0.0, ccp_ref[...] * chp_ref[...])
    row = lax.broadcasted_iota(jnp.int32, up.shape, 0)

    def shifted(k):
        uk = pltpu.roll(u, k, 0)
        top = jnp.where(row < k, pltpu.roll(up, k, 0), uk[0:SUBLANES, :])
        return jnp.concatenate([top, uk[SUBLANES:, :]], axis=0)

    cw = cw_ref[...]
    y = cw[2:3, :] * u + cw[1:2, :] * shifted(1) + cw[0:1, :] * shifted(2)
    cv = _rms(cb_ref[...] * y) * cg_ref[...]
    both = jnp.concatenate([att_ref[...], cv.astype(BF16)], axis=-1)
    mix = jnp.dot(both, wo_ref[...], preferred_element_type=F32)
    x1 = x_ref[...] + mod_ref[2:3, :] * mix
    x1_ref[...] = x1
    h2 = (_rms(x1) * g2_ref[...]) * (1.0 + mod_ref[4:5, :]) + mod_ref[3:4, :]
    h2t = h2.T.astype(BF16)
    h2t_ref[...] = h2t
    qpt_ref[...] = jnp.dot(wqt_ref[...], h2t, preferred_element_type=F32).astype(BF16)


def _mix(x2, att2, cbh, conv_w, conv_g, w_o, mod3, g2, wq_t, S):
    T, D = x2.shape
    C = conv_w.shape[1]
    NQ = wq_t.shape[0]
    per_b = S // TOK_BLK
    rows8 = TOK_BLK // SUBLANES
    prev = lambda col: pl.BlockSpec((SUBLANES, C), lambda i: (jnp.maximum(i * rows8 - 1, 0), col))
    full = lambda a: pl.BlockSpec(a.shape, lambda i: (0,) * a.ndim)
    return pl.pallas_call(
        functools.partial(_mix_kernel, blocks_per_seq=per_b),
        grid=(T // TOK_BLK,),
        in_specs=[pl.BlockSpec((TOK_BLK, D), lambda i: (i, 0)),
                  pl.BlockSpec((TOK_BLK, ATT_WIDTH), lambda i: (i, 0)),
                  pl.BlockSpec((TOK_BLK, C), lambda i: (i, 0)),
                  pl.BlockSpec((TOK_BLK, C), lambda i: (i, 1)),
                  pl.BlockSpec((TOK_BLK, C), lambda i: (i, 2)),
                  prev(1), prev(2),
                  full(conv_w), full(conv_g), full(w_o),
                  pl.BlockSpec((None, 6, D), lambda i: (i // per_b, 0, 0)),
                  full(g2), full(wq_t)],
        out_specs=[pl.BlockSpec((TOK_BLK, D), lambda i: (i, 0)),
                   pl.BlockSpec((D, TOK_BLK), lambda i: (0, i)),
                   pl.BlockSpec((NQ, TOK_BLK), lambda i: (0, i))],
        out_shape=[jax.ShapeDtypeStruct((T, D), F32),
                   jax.ShapeDtypeStruct((D, T), BF16),
                   jax.ShapeDtypeStruct((NQ, T), BF16)],
        compiler_params=_cparams(("arbitrary",)),
        name="mix",
    )(x2, att2, cbh, cbh, cbh, cbh, cbh, conv_w, conv_g, w_o, mod3, g2, wq_t)


_PAIR_ROWS = ((0, 8, 0), (0, 8, 8), (1, 8, 0), (2, 5, 0), (3, 4, 0), (4, 3, 0),
              (5, 2, 0), (6, 2, 0), (7, 2, 0))


def _batcher_pairs(n):
    pairs = []
    p = 1
    while p < n:
        k = p
        while k >= 1:
            for j in range(k % p, n - k, 2 * k):
                for i in range(min(k, n - j - k)):
                    if (i + j) // (2 * p) == (i + j + k) // (2 * p):
                        pairs.append((i + j, i + j + k))
            k //= 2
        p *= 2
    return pairs


def _sorted_top16(x):
    v = [x[i * SUBLANES:(i + 1) * SUBLANES, :] for i in range(PEER_TOPK)]

    def exchange(i, j):
        v[i], v[j] = jnp.maximum(v[i], v[j]), jnp.minimum(v[i], v[j])

    for i, j in _batcher_pairs(PEER_TOPK):
        exchange(i, j)
    shift = SUBLANES // 2
    while shift >= 1:
        other = [pltpu.roll(t, shift, 0) for t in v]
        v = [jnp.maximum(v[i], other[PEER_TOPK - 1 - i]) for i in range(PEER_TOPK)]
        d = PEER_TOPK // 2
        while d >= 1:
            for i in range(PEER_TOPK):
                if i & d == 0:
                    exchange(i, i + d)
            d //= 2
        shift //= 2
    return v


def _select_kernel(qpt_ref, keys_ref, n1_ref, e1_ref, rf_ref, sc_ref, a_ref, b_ref):
    TB = qpt_ref.shape[1]
    for hp in range(2 * PEER_HEADS):
        sc_ref[hp] = jnp.dot(keys_ref[hp], qpt_ref[hp * N_KEYS:(hp + 1) * N_KEYS, :],
                             preferred_element_type=F32)

    row8 = lax.broadcasted_iota(jnp.int32, (SUBLANES, LANES), 0)
    row16 = lax.broadcasted_iota(jnp.int32, (PEER_TOPK, LANES), 0)

    def per_head(h, carry):
        for lt in range(TB // LANES):
            lanes = slice(lt * LANES, (lt + 1) * LANES)
            s1 = sc_ref[2 * h, :, lanes]
            s2 = sc_ref[2 * h + 1, :, lanes]
            top1 = _sorted_top16(s1)
            top2 = _sorted_top16(s2)
            for r in range(PEER_TOPK):
                a_ref[r:r + 1, :] = top1[r][0:1, :]
                b_ref[r:r + 1, :] = top2[r][0:1, :]
            chunks = [s2[i * SUBLANES:(i + 1) * SUBLANES, :] for i in range(N_KEYS // SUBLANES)]
            ranks = [jnp.zeros_like(ch) for ch in chunks]
            for r in range(PEER_TOPK):
                ranks = [jnp.where(top2[r] > ch, float(r + 1), rk) for ch, rk in zip(chunks, ranks)]
            rank2 = jnp.concatenate(ranks, axis=0)
            a = a_ref[...]
            b = b_ref[...]
            ea = jnp.exp(a - a[0:1, :])
            eb = jnp.exp(b - b[0:1, :])
            cands, prods = [], []
            for (i, n, j0) in _PAIR_ROWS:
                cs = a[i:i + 1, :] + b[j0:j0 + SUBLANES, :]
                ps = ea[i:i + 1, :] * eb[j0:j0 + SUBLANES, :]
                if n < SUBLANES:
                    cs = jnp.where(row8 < n, cs, -jnp.inf)
                cands.append(cs)
                prods.append(ps)
            cands.append(a[SUBLANES:, :] + b[0:1, :])
            prods.append(ea[SUBLANES:, :] * eb[0:1, :])
            cand = jnp.concatenate(cands, axis=0)
            prod = jnp.concatenate(prods, axis=0)
            pad = jnp.full((PEER_TOPK * SUBLANES - cand.shape[0], LANES), -jnp.inf, F32)
            tau = _sorted_top16(jnp.concatenate([cand, pad], axis=0))[PEER_TOPK - 1][0:1, :]
            z = jnp.sum(jnp.where(cand >= tau, prod, 0.0), axis=0, keepdims=True)
            n1 = jnp.zeros_like(s1)
            for j in range(4):
                n1 = n1 + jnp.where(s1 + b[j:j + 1, :] >= tau, 1.0, 0.0)
            extra = jnp.zeros_like(s1)
            for k in reversed(range(3)):
                beyond = jnp.where((a[k:k + 1, :] + b >= tau) & (row16 >= 4), 1.0, 0.0)
                extra = jnp.where(s1 == a[k:k + 1, :], jnp.sum(beyond, axis=0, keepdims=True), extra)
            n1 = n1 + extra
            n1_ref[h, :, lanes] = n1
            e1_ref[h, :, lanes] = 0.5 * jnp.exp(s1 - a[0:1, :])
            r2 = pltpu.bitcast(rank2.astype(BF16), jnp.int32)
            f2 = pltpu.bitcast((jnp.exp(s2 - b[0:1, :]) / z).astype(BF16), jnp.int32)
            for k in range(N_KEYS // BF16_ROWS):
                ks = slice(k * SUBLANES, (k + 1) * SUBLANES)
                rf_ref[h, lt, 2 * k] = r2[ks, :]
                rf_ref[h, lt, 2 * k + 1] = f2[ks, :]
        return carry

    lax.fori_loop(0, PEER_HEADS, per_head, 0)


def _select(qp_t, keys):
    NQ, T = qp_t.shape
    H = PEER_HEADS
    rows = pl.BlockSpec((H, N_KEYS, TOK_BLK), lambda i: (0, 0, i))
    rows_shape = jax.ShapeDtypeStruct((H, N_KEYS, T), F32)
    n_vregs = 2 * (N_KEYS // BF16_ROWS)
    tiles = pl.BlockSpec((H, TOK_BLK // LANES, n_vregs, SUBLANES, LANES), lambda i: (0, i, 0, 0, 0))
    tiles_shape = jax.ShapeDtypeStruct((H, T // LANES, n_vregs, SUBLANES, LANES), jnp.int32)
    return pl.pallas_call(
        _select_kernel,
        grid=(T // TOK_BLK,),
        in_specs=[pl.BlockSpec((NQ, TOK_BLK), lambda i: (0, i)),
                  pl.BlockSpec(keys.shape, lambda i: (0, 0, 0))],
        out_specs=[rows, rows, tiles],
        out_shape=[rows_shape, rows_shape, tiles_shape],
        scratch_shapes=[pltpu.VMEM((2 * H, N_KEYS, TOK_BLK), F32),
                        pltpu.VMEM((PEER_TOPK, LANES), F32),
                        pltpu.VMEM((PEER_TOPK, LANES), F32)],
        compiler_params=_cparams(("arbitrary",)),
        name="select",
    )(qp_t, keys)


def _expert_kernel(h2t_ref, u_ref, vt_ref, n1_ref, e1_ref, rf_ref,
                   x1_ref, mod_ref, fg_ref, o_ref, acc_ref, a_ref, p_ref):
    c = pl.program_id(1)
    EC, TB = a_ref.shape

    @pl.when(c == 0)
    def _():
        acc_ref[...] = jnp.zeros_like(acc_ref)

    groups = [slice(g * EXPERT_GRP, (g + 1) * EXPERT_GRP) for g in range(EC // EXPERT_GRP)]
    for grp in groups:
        a_ref[grp, :] = jnp.dot(u_ref[grp, :], h2t_ref[...], preferred_element_type=F32)

    for j in range(EC // N_KEYS):
        for lt in range(TB // LANES):
            lanes = slice(lt * LANES, (lt + 1) * LANES)
            w = [jnp.zeros((BF16_ROWS, LANES), BF16) for _ in range(N_KEYS // BF16_ROWS)]
            for h in range(PEER_HEADS):
                n1 = jnp.broadcast_to(n1_ref[h, j:j + 1, lanes], (BF16_ROWS, LANES)).astype(BF16)
                e1 = jnp.broadcast_to(e1_ref[h, j:j + 1, lanes], (BF16_ROWS, LANES)).astype(BF16)
                for k in range(len(w)):
                    hit = pltpu.bitcast(rf_ref[h, lt, 2 * k], BF16) < n1
                    f2 = pltpu.bitcast(rf_ref[h, lt, 2 * k + 1], BF16)
                    w[k] = w[k] + jnp.where(hit, f2 * e1, jnp.zeros_like(e1))
            for k in range(len(w)):
                rows = slice(j * N_KEYS + k * BF16_ROWS, j * N_KEYS + (k + 1) * BF16_ROWS)
                a = a_ref[rows, lanes]
                gelu2 = a * (1.0 + lax.erf(a * INV_SQRT2))
                p_ref[rows, lanes] = w[k] * gelu2.astype(BF16)

    update = None
    for grp in groups:
        part = jnp.dot(vt_ref[:, grp], p_ref[grp, :], preferred_element_type=F32)
        update = part if update is None else update + part
    acc_ref[...] += update

    @pl.when(c == pl.num_programs(1) - 1)
    def _():
        x2 = x1_ref[...] + mod_ref[5:6, :] * acc_ref[...].T
        o_ref[...] = _rms(x2) * fg_ref[...]


def _experts(h2_t, u_b, v_t, n1, e1, rf, x1, mod3, final_g, S):
    D, T = h2_t.shape
    E = u_b.shape[0]
    H = PEER_HEADS
    TB, EC = TOK_BLK, EXPERT_BLK
    per_b = S // TB
    rows = pl.BlockSpec((H, EC // N_KEYS, TB), lambda i, c: (0, c, i))
    tiles = pl.BlockSpec((H, TB // LANES) + rf.shape[2:], lambda i, c: (0, i, 0, 0, 0))
    return pl.pallas_call(
        _expert_kernel,
        grid=(T // TB, E // EC),
        in_specs=[pl.BlockSpec((D, TB), lambda i, c: (0, i)),
                  pl.BlockSpec((EC, D), lambda i, c: (c, 0)),
                  pl.BlockSpec((D, EC), lambda i, c: (0, c)),
                  rows, rows, tiles,
                  pl.BlockSpec((TB, D), lambda i, c: (i, 0)),
                  pl.BlockSpec((None, 6, D), lambda i, c: (i // per_b, 0, 0)),
                  pl.BlockSpec((1, D), lambda i, c: (0, 0))],
        out_specs=pl.BlockSpec((TB, D), lambda i, c: (i, 0)),
        out_shape=jax.ShapeDtypeStruct((T, D), F32),
        scratch_shapes=[pltpu.VMEM((D, TB), F32),
                        pltpu.VMEM((EC, TB), F32),
                        pltpu.VMEM((EC, TB), BF16)],
        compiler_params=_cparams(("arbitrary", "arbitrary")),
        name="experts",
    )(h2_t, u_b, v_t, n1, e1, rf, x1, mod3, final_g)


def kernel(x, c, ada_w, ada_b, norm1_g, w_in, conv_w, lam_q1, lam_k1, lam_q2, lam_k2, attn_norm_g, conv_norm_g, w_o, norm2_g, peer_wq, peer_keys, peer_u, peer_v, final_g):
    B, S, D = x.shape
    T = B * S
    assert ada_w.shape[0] == 1, "single-layer block"
    assert S % TOK_BLK == 0 and S % Q_BLK == 0 and Q_BLK % CHUNK == 0
    assert peer_u.shape[1] % EXPERT_BLK == 0 and (EXPERT_BLK // N_KEYS) % SUBLANES == 0

    x2 = x.reshape(T, D)
    mod3 = _modulation(c, ada_w[0], ada_b).reshape(B, 6, D)

    qkv, cbh = _inproj(x2, mod3, norm1_g, w_in[0].astype(BF16), S)

    lam_vecs = jnp.concatenate([lam_q1, lam_k1, lam_q2, lam_k2], axis=0)
    att = _attention(qkv.reshape(B, S, -1), lam_vecs, attn_norm_g)

    x1, h2_t, qp_t = _mix(x2, att.reshape(T, ATT_WIDTH), cbh, conv_w[0], conv_norm_g,
                          w_o[0].astype(BF16), mod3, norm2_g, peer_wq[0].T.astype(BF16), S)

    keys = peer_keys[0].reshape(2 * PEER_HEADS, N_KEYS, -1).astype(BF16)
    n1, e1, rf = _select(qp_t, keys)

    out = _experts(h2_t, peer_u[0].astype(BF16), peer_v[0].T.astype(BF16),
                   n1, e1, rf, x1, mod3, final_g.reshape(1, D), S)
    return out.reshape(B, S, D)
```

```python
import functools
import math

import jax
import jax.numpy as jnp
from jax import lax
from jax.experimental import pallas as pl
from jax.experimental.pallas import tpu as pltpu

F32 = jnp.float32
BF16 = jnp.bfloat16

EPS = 1e-6
CHUNK = 64
ATT_HEADS = 4
ATT_HEAD_DIM = 64
ATT_V_DIM = 128
ATT_WIDTH = ATT_HEADS * ATT_V_DIM
CONV_K = 3
PEER_HEADS = 8
N_KEYS = 128
PEER_TOPK = 16
LAM_INIT = 0.8 - 0.6 * math.exp(0.0)
INV_SQRT2 = 0.7071067811865476

LANES = 128
SUBLANES = 8
BF16_ROWS = 2 * SUBLANES
Q_BLK = 256
TOK_BLK = 512
EXPERT_BLK = 2048
EXPERT_GRP = 256
VMEM_LIMIT = 48 * 1024 * 1024


def _cparams(sem):
    return pltpu.CompilerParams(dimension_semantics=sem, vmem_limit_bytes=VMEM_LIMIT)


def _rms(x):
    return x * lax.rsqrt(jnp.mean(x * x, axis=-1, keepdims=True) + EPS)


def _mod_kernel(c_ref, w_ref, b_ref, o_ref):
    sc = jax.nn.silu(c_ref[...])
    o_ref[...] = jnp.dot(sc.astype(BF16), w_ref[...].astype(BF16),
                         preferred_element_type=F32) + b_ref[...]


def _modulation(c, ada_w, ada_b):
    B, D = c.shape
    N = ada_w.shape[1]
    return pl.pallas_call(
        _mod_kernel,
        grid=(N // D,),
        in_specs=[pl.BlockSpec((B, D), lambda j: (0, 0)),
                  pl.BlockSpec((D, D), lambda j: (0, j)),
                  pl.BlockSpec((1, D), lambda j: (0, j))],
        out_specs=pl.BlockSpec((B, D), lambda j: (0, j)),
        out_shape=jax.ShapeDtypeStruct((B, N), F32),
        compiler_params=_cparams(("arbitrary",)),
        name="mod",
    )(c, ada_w, ada_b)


def _inproj_kernel(x_ref, mod_ref, g_ref, w_ref, qkv_ref, cbh_ref):
    y = _rms(x_ref[...])
    h = (y * g_ref[...]) * (1.0 + mod_ref[1:2, :]) + mod_ref[0:1, :]
    p = jnp.dot(h.astype(BF16), w_ref[...], preferred_element_type=F32)
    n_att = qkv_ref.shape[1]
    qkv_ref[...] = p[:, :n_att].astype(BF16)
    cbh_ref[...] = p[:, n_att:]


def _inproj(x2, mod3, g, w, S):
    T, D = x2.shape
    n_att = 3 * ATT_WIDTH
    n_conv = w.shape[1] - n_att
    per_b = S // TOK_BLK
    return pl.pallas_call(
        _inproj_kernel,
        grid=(T // TOK_BLK,),
        in_specs=[pl.BlockSpec((TOK_BLK, D), lambda i: (i, 0)),
                  pl.BlockSpec((None, 6, D), lambda i: (i // per_b, 0, 0)),
                  pl.BlockSpec((1, D), lambda i: (0, 0)),
                  pl.BlockSpec(w.shape, lambda i: (0, 0))],
        out_specs=[pl.BlockSpec((TOK_BLK, n_att), lambda i: (i, 0)),
                   pl.BlockSpec((TOK_BLK, n_conv), lambda i: (i, 0))],
        out_shape=[jax.ShapeDtypeStruct((T, n_att), BF16),
                   jax.ShapeDtypeStruct((T, n_conv), F32)],
        compiler_params=_cparams(("arbitrary",)),
        name="inproj",
    )(x2, mod3, g, w)


def _attn_kernel(q_ref, k_ref, v_ref, lam_ref, g_ref, o_ref, bias_ref, vext_ref):
    S = q_ref.shape[0]
    h = pl.program_id(1)
    vext_ref[:, :ATT_V_DIM] = v_ref[...]
    vext_ref[:, ATT_V_DIM:] = jnp.ones((S, ATT_V_DIM), BF16)

    @pl.when(pl.program_id(0) == 0)
    def _():
        slope = jnp.where(h == 0, 2.0 ** -2,
                          jnp.where(h == 1, 2.0 ** -4,
                                    jnp.where(h == 2, 2.0 ** -6, 2.0 ** -8))).astype(F32)
        r = lax.broadcasted_iota(jnp.int32, (Q_BLK, S), 0) + (S - Q_BLK)
        x = lax.broadcasted_iota(jnp.int32, (Q_BLK, S), 1)
        bias = -slope * jnp.abs(r - x).astype(F32)
        allowed = (x // CHUNK) <= (r // CHUNK)
        bias_ref[h] = jnp.where(allowed, bias, -jnp.inf)

    lv = lam_ref[...]
    lam = (jnp.exp(jnp.sum(lv[0:1, :] * lv[1:2, :], axis=-1, keepdims=True))
           - jnp.exp(jnp.sum(lv[2:3, :] * lv[3:4, :], axis=-1, keepdims=True))
           + LAM_INIT)
    gain = g_ref[...] * (1.0 - LAM_INIT)
    nt = (((1,), (1,)), ((), ()))
    for j in range(S // Q_BLK):
        q0 = j * Q_BLK
        L = q0 + Q_BLK
        qb = q_ref[q0:q0 + Q_BLK, :]
        lane = lax.broadcasted_iota(jnp.int32, qb.shape, 1)
        scale = jnp.asarray(ATT_HEAD_DIM ** -0.5, BF16)
        zero = jnp.zeros_like(qb)
        q_lo = jnp.where(lane < ATT_HEAD_DIM, qb, zero) * scale
        q_hi = jnp.where(lane >= ATT_HEAD_DIM, qb, zero) * scale
        kk = k_ref[0:L, :]
        vv = vext_ref[0:L, :]

        def half(qh):
            s = lax.dot_general(qh, kk, nt, preferred_element_type=F32) + bias_ref[h, :, S - L:S]
            m = jnp.max(s, axis=-1, keepdims=True)
            p = jnp.exp(s - m)
            o = jnp.dot(p.astype(BF16), vv, preferred_element_type=F32)
            return o[:, :ATT_V_DIM] / o[:, ATT_V_DIM:ATT_V_DIM + 1]

        o = half(q_lo) - lam * half(q_hi)
        o_ref[q0:q0 + Q_BLK, :] = (_rms(o) * gain).astype(o_ref.dtype)


def _attention(qkv3, lam_vecs, attn_g):
    B, S, _ = qkv3.shape
    dv = ATT_V_DIM
    blk = lambda off: pl.BlockSpec((None, S, dv), lambda b, h: (b, 0, off + h))
    return pl.pallas_call(
        _attn_kernel,
        grid=(B, ATT_HEADS),
        in_specs=[blk(0), blk(ATT_HEADS), blk(2 * ATT_HEADS),
                  pl.BlockSpec(lam_vecs.shape, lambda b, h: (0, 0)),
                  pl.BlockSpec((1, dv), lambda b, h: (0, h))],
        out_specs=pl.BlockSpec((None, S, dv), lambda b, h: (b, 0, h)),
        out_shape=jax.ShapeDtypeStruct((B, S, ATT_WIDTH), BF16),
        scratch_shapes=[pltpu.VMEM((ATT_HEADS, Q_BLK, S), F32),
                        pltpu.VMEM((S, 2 * ATT_V_DIM), BF16)],
        compiler_params=_cparams(("arbitrary", "arbitrary")),
        name="attn",
    )(qkv3, qkv3, qkv3, lam_vecs, attn_g)


def _mix_kernel(x_ref, att_ref, cb_ref, cc_ref, ch_ref, ccp_ref, chp_ref, cw_ref, cg_ref,
                wo_ref, mod_ref, g2_ref, wqt_ref, x1_ref, h2t_ref, qpt_ref, *, blocks_per_seq):
    i = pl.program_id(0)
    u = cc_ref[...] * ch_ref[...]
    first = (i % blocks_per_seq) == 0
    up = jnp.where(first, 0.0, ccp_ref[...] * chp_ref[...])
    row = lax.broadcasted_iota(jnp.int32, up.shape, 0)

    def shifted(k):
        uk = pltpu.roll(u, k, 0)
        top = jnp.where(row < k, pltpu.roll(up, k, 0), uk[0:SUBLANES, :])
        return jnp.concatenate([top, uk[SUBLANES:, :]], axis=0)

    cw = cw_ref[...]
    y = cw[2:3, :] * u + cw[1:2, :] * shifted(1) + cw[0:1, :] * shifted(2)
    cv = _rms(cb_ref[...] * y) * cg_ref[...]
    both = jnp.concatenate([att_ref[...], cv.astype(BF16)], axis=-1)
    mix = jnp.dot(both, wo_ref[...], preferred_element_type=F32)
    x1 = x_ref[...] + mod_ref[2:3, :] * mix
    x1_ref[...] = x1
    h2 = (_rms(x1) * g2_ref[...]) * (1.0 + mod_ref[4:5, :]) + mod_ref[3:4, :]
    h2t = h2.T.astype(BF16)
    h2t_ref[...] = h2t
    qpt_ref[...] = jnp.dot(wqt_ref[...], h2t, preferred_element_type=F32).astype(BF16)


def _mix(x2, att2, cbh, conv_w, conv_g, w_o, mod3, g2, wq_t, S):
    T, D = x2.shape
    C = conv_w.shape[1]
    NQ = wq_t.shape[0]
    per_b = S // TOK_BLK
    rows8 = TOK_BLK // SUBLANES
    prev = lambda col: pl.BlockSpec((SUBLANES, C), lambda i: (jnp.maximum(i * rows8 - 1, 0), col))
    full = lambda a: pl.BlockSpec(a.shape, lambda i: (0,) * a.ndim)
    return pl.pallas_call(
        functools.partial(_mix_kernel, blocks_per_seq=per_b),
        grid=(T // TOK_BLK,),
        in_specs=[pl.BlockSpec((TOK_BLK, D), lambda i: (i, 0)),
                  pl.BlockSpec((TOK_BLK, ATT_WIDTH), lambda i: (i, 0)),
                  pl.BlockSpec((TOK_BLK, C), lambda i: (i, 0)),
                  pl.BlockSpec((TOK_BLK, C), lambda i: (i, 1)),
                  pl.BlockSpec((TOK_BLK, C), lambda i: (i, 2)),
                  prev(1), prev(2),
                  full(conv_w), full(conv_g), full(w_o),
                  pl.BlockSpec((None, 6, D), lambda i: (i // per_b, 0, 0)),
                  full(g2), full(wq_t)],
        out_specs=[pl.BlockSpec((TOK_BLK, D), lambda i: (i, 0)),
                   pl.BlockSpec((D, TOK_BLK), lambda i: (0, i)),
                   pl.BlockSpec((NQ, TOK_BLK), lambda i: (0, i))],
        out_shape=[jax.ShapeDtypeStruct((T, D), F32),
                   jax.ShapeDtypeStruct((D, T), BF16),
                   jax.ShapeDtypeStruct((NQ, T), BF16)],
        compiler_params=_cparams(("arbitrary",)),
        name="mix",
    )(x2, att2, cbh, cbh, cbh, cbh, cbh, conv_w, conv_g, w_o, mod3, g2, wq_t)


_PAIR_ROWS = ((0, 8, 0), (0, 8, 8), (1, 8, 0), (2, 5, 0), (3, 4, 0), (4, 3, 0),
              (5, 2, 0), (6, 2, 0), (7, 2, 0))


def _batcher_pairs(n):
    pairs = []
    p = 1
    while p < n:
        k = p
        while k >= 1:
            for j in range(k % p, n - k, 2 * k):
                for i in range(min(k, n - j - k)):
                    if (i + j) // (2 * p) == (i + j + k) // (2 * p):
                        pairs.append((i + j, i + j + k))
            k //= 2
        p *= 2
    return pairs


def _sorted_top16(x):
    v = [x[i * SUBLANES:(i + 1) * SUBLANES, :] for i in range(PEER_TOPK)]

    def exchange(i, j):
        v[i], v[j] = jnp.maximum(v[i], v[j]), jnp.minimum(v[i], v[j])

    for i, j in _batcher_pairs(PEER_TOPK):
        exchange(i, j)
    shift = SUBLANES // 2
    while shift >= 1:
        other = [pltpu.roll(t, shift, 0) for t in v]
        v = [jnp.maximum(v[i], other[PEER_TOPK - 1 - i]) for i in range(PEER_TOPK)]
        d = PEER_TOPK // 2
        while d >= 1:
            for i in range(PEER_TOPK):
                if i & d == 0:
                    exchange(i, i + d)
            d //= 2
        shift //= 2
    return v


def _select_kernel(qpt_ref, keys_ref, n1_ref, e1_ref, rf_ref, sc_ref, a_ref, b_ref):
    TB = qpt_ref.shape[1]
    for hp in range(2 * PEER_HEADS):
        sc_ref[hp] = jnp.dot(keys_ref[hp], qpt_ref[hp * N_KEYS:(hp + 1) * N_KEYS, :],
                             preferred_element_type=F32)

    row8 = lax.broadcasted_iota(jnp.int32, (SUBLANES, LANES), 0)
    row16 = lax.broadcasted_iota(jnp.int32, (PEER_TOPK, LANES), 0)

    def per_head(h, carry):
        for lt in range(TB // LANES):
            lanes = slice(lt * LANES, (lt + 1) * LANES)
            s1 = sc_ref[2 * h, :, lanes]
            s2 = sc_ref[2 * h + 1, :, lanes]
            top1 = _sorted_top16(s1)
            top2 = _sorted_top16(s2)
            for r in range(PEER_TOPK):
                a_ref[r:r + 1, :] = top1[r][0:1, :]
                b_ref[r:r + 1, :] = top2[r][0:1, :]
            chunks = [s2[i * SUBLANES:(i + 1) * SUBLANES, :] for i in range(N_KEYS // SUBLANES)]
            ranks = [jnp.zeros_like(ch) for ch in chunks]
            for r in range(PEER_TOPK):
                ranks = [jnp.where(top2[r] > ch, float(r + 1), rk) for ch, rk in zip(chunks, ranks)]
            rank2 = jnp.concatenate(ranks, axis=0)
            a = a_ref[...]
            b = b_ref[...]
            ea = jnp.exp(a - a[0:1, :])
            eb = jnp.exp(b - b[0:1, :])
            cands, prods = [], []
            for (i, n, j0) in _PAIR_ROWS:
                cs = a[i:i + 1, :] + b[j0:j0 + SUBLANES, :]
                ps = ea[i:i + 1, :] * eb[j0:j0 + SUBLANES, :]
                if n < SUBLANES:
                    cs = jnp.where(row8 < n, cs, -jnp.inf)
                cands.append(cs)
                prods.append(ps)
            cands.append(a[SUBLANES:, :] + b[0:1, :])
            prods.append(ea[SUBLANES:, :] * eb[0:1, :])
            cand = jnp.concatenate(cands, axis=0)
            prod = jnp.concatenate(prods, axis=0)
            pad = jnp.full((PEER_TOPK * SUBLANES - cand.shape[0], LANES), -jnp.inf, F32)
            tau = _sorted_top16(jnp.concatenate([cand, pad], axis=0))[PEER_TOPK - 1][0:1, :]
            z = jnp.sum(jnp.where(cand >= tau, prod, 0.0), axis=0, keepdims=True)
            n1 = jnp.zeros_like(s1)
            for j in range(4):
                n1 = n1 + jnp.where(s1 + b[j:j + 1, :] >= tau, 1.0, 0.0)
            extra = jnp.zeros_like(s1)
            for k in reversed(range(3)):
                beyond = jnp.where((a[k:k + 1, :] + b >= tau) & (row16 >= 4), 1.0, 0.0)
                extra = jnp.where(s1 == a[k:k + 1, :], jnp.sum(beyond, axis=0, keepdims=True), extra)
            n1 = n1 + extra
            n1_ref[h, :, lanes] = n1
            e1_ref[h, :, lanes] = 0.5 * jnp.exp(s1 - a[0:1, :])
            r2 = pltpu.bitcast(rank2.astype(BF16), jnp.int32)
            f2 = pltpu.bitcast((jnp.exp(s2 - b[0:1, :]) / z).astype(BF16), jnp.int32)
            for k in range(N_KEYS // BF16_ROWS):
                ks = slice(k * SUBLANES, (k + 1) * SUBLANES)
                rf_ref[h, lt, 2 * k] = r2[ks, :]
                rf_ref[h, lt, 2 * k + 1] = f2[ks, :]
        return carry

    lax.fori_loop(0, PEER_HEADS, per_head, 0)


def _select(qp_t, keys):
    NQ, T = qp_t.shape
    H = PEER_HEADS
    rows = pl.BlockSpec((H, N_KEYS, TOK_BLK), lambda i: (0, 0, i))
    rows_shape = jax.ShapeDtypeStruct((H, N_KEYS, T), F32)
    n_vregs = 2 * (N_KEYS // BF16_ROWS)
    tiles = pl.BlockSpec((H, TOK_BLK // LANES, n_vregs, SUBLANES, LANES), lambda i: (0, i, 0, 0, 0))
    tiles_shape = jax.ShapeDtypeStruct((H, T // LANES, n_vregs, SUBLANES, LANES), jnp.int32)
    return pl.pallas_call(
        _select_kernel,
        grid=(T // TOK_BLK,),
        in_specs=[pl.BlockSpec((NQ, TOK_BLK), lambda i: (0, i)),
                  pl.BlockSpec(keys.shape, lambda i: (0, 0, 0))],
        out_specs=[rows, rows, tiles],
        out_shape=[rows_shape, rows_shape, tiles_shape],
        scratch_shapes=[pltpu.VMEM((2 * H, N_KEYS, TOK_BLK), F32),
                        pltpu.VMEM((PEER_TOPK, LANES), F32),
                        pltpu.VMEM((PEER_TOPK, LANES), F32)],
        compiler_params=_cparams(("arbitrary",)),
        name="select",
    )(qp_t, keys)


def _expert_kernel(h2t_ref, u_ref, vt_ref, n1_ref, e1_ref, rf_ref,
                   x1_ref, mod_ref, fg_ref, o_ref, acc_ref, a_ref, p_ref, *w_refs):
    c = pl.program_id(1)
    EC, TB = a_ref.shape

    @pl.when(c == 0)
    def _():
        acc_ref[...] = jnp.zeros_like(acc_ref)

    groups = [slice(g * EXPERT_GRP, (g + 1) * EXPERT_GRP) for g in range(EC // EXPERT_GRP)]
    for grp in groups:
        a_ref[grp, :] = jnp.dot(u_ref[grp, :], h2t_ref[...], preferred_element_type=F32)

    zero = jnp.minimum(pl.program_id(1), 0)
    keys_per_grp = EXPERT_GRP // N_KEYS
    tiles_per_key = N_KEYS // BF16_ROWS

    def w_rows(jj, k):
        return pl.ds(pl.multiple_of(zero + (jj * N_KEYS + k * BF16_ROWS), BF16_ROWS), BF16_ROWS)

    for g, w_ref in enumerate(w_refs):
        for jj in range(keys_per_grp):
            j = g * keys_per_grp + jj
            for lt in range(TB // LANES):
                lanes = slice(lt * LANES, (lt + 1) * LANES)
                w = [jnp.zeros((BF16_ROWS, LANES), BF16) for _ in range(tiles_per_key)]
                for h in range(PEER_HEADS):
                    n1 = jnp.broadcast_to(n1_ref[h, j:j + 1, lanes], (BF16_ROWS, LANES)).astype(BF16)
                    e1 = jnp.broadcast_to(e1_ref[h, j:j + 1, lanes], (BF16_ROWS, LANES)).astype(BF16)
                    for k in range(tiles_per_key):
                        hit = pltpu.bitcast(rf_ref[h, lt, 2 * k], BF16) < n1
                        f2 = pltpu.bitcast(rf_ref[h, lt, 2 * k + 1], BF16)
                        w[k] = w[k] + jnp.where(hit, f2 * e1, jnp.zeros_like(e1))
                for k in range(tiles_per_key):
                    w_ref[w_rows(jj, k), lanes] = w[k]

    for g, w_ref in enumerate(w_refs):
        for jj in range(keys_per_grp):
            j = g * keys_per_grp + jj
            for lt in range(TB // LANES):
                lanes = slice(lt * LANES, (lt + 1) * LANES)
                for k in range(tiles_per_key):
                    rows = slice(j * N_KEYS + k * BF16_ROWS, j * N_KEYS + (k + 1) * BF16_ROWS)
                    a = a_ref[rows, lanes]
                    gelu2 = a * (1.0 + lax.erf(a * INV_SQRT2))
                    p_ref[rows, lanes] = w_ref[w_rows(jj, k), lanes] * gelu2.astype(BF16)

    update = None
    for grp in groups:
        part = jnp.dot(vt_ref[:, grp], p_ref[grp, :], preferred_element_type=F32)
        update = part if update is None else update + part
    acc_ref[...] += update

    @pl.when(c == pl.num_programs(1) - 1)
    def _():
        x2 = x1_ref[...] + mod_ref[5:6, :] * acc_ref[...].T
        o_ref[...] = _rms(x2) * fg_ref[...]


def _experts(h2_t, u_b, v_t, n1, e1, rf, x1, mod3, final_g, S):
    D, T = h2_t.shape
    E = u_b.shape[0]
    H = PEER_HEADS
    TB, EC = TOK_BLK, EXPERT_BLK
    per_b = S // TB
    rows = pl.BlockSpec((H, EC // N_KEYS, TB), lambda i, c: (0, c, i))
    tiles = pl.BlockSpec((H, TB // LANES) + rf.shape[2:], lambda i, c: (0, i, 0, 0, 0))
    return pl.pallas_call(
        _expert_kernel,
        grid=(T // TB, E // EC),
        in_specs=[pl.BlockSpec((D, TB), lambda i, c: (0, i)),
                  pl.BlockSpec((EC, D), lambda i, c: (c, 0)),
                  pl.BlockSpec((D, EC), lambda i, c: (0, c)),
                  rows, rows, tiles,
                  pl.BlockSpec((TB, D), lambda i, c: (i, 0)),
                  pl.BlockSpec((None, 6, D), lambda i, c: (i // per_b, 0, 0)),
                  pl.BlockSpec((1, D), lambda i, c: (0, 0))],
        out_specs=pl.BlockSpec((TB, D), lambda i, c: (i, 0)),
        out_shape=jax.ShapeDtypeStruct((T, D), F32),
        scratch_shapes=[pltpu.VMEM((D, TB), F32),
                        pltpu.VMEM((EC, TB), F32),
                        pltpu.VMEM((EC, TB), BF16)]
                       + [pltpu.VMEM((EXPERT_GRP, TB), BF16)] * (EC // EXPERT_GRP),
        compiler_params=_cparams(("arbitrary", "arbitrary")),
        name="experts",
    )(h2_t, u_b, v_t, n1, e1, rf, x1, mod3, final_g)


def kernel(x, c, ada_w, ada_b, norm1_g, w_in, conv_w, lam_q1, lam_k1, lam_q2, lam_k2, attn_norm_g, conv_norm_g, w_o, norm2_g, peer_wq, peer_keys, peer_u, peer_v, final_g):
    B, S, D = x.shape
    T = B * S
    assert ada_w.shape[0] == 1, "single-layer block"
    assert S % TOK_BLK == 0 and S % Q_BLK == 0 and Q_BLK % CHUNK == 0
    assert peer_u.shape[1] % EXPERT_BLK == 0 and (EXPERT_BLK // N_KEYS) % SUBLANES == 0

    x2 = x.reshape(T, D)
    mod3 = _modulation(c, ada_w[0], ada_b).reshape(B, 6, D)

    qkv, cbh = _inproj(x2, mod3, norm1_g, w_in[0].astype(BF16), S)

    lam_vecs = jnp.concatenate([lam_q1, lam_k1, lam_q2, lam_k2], axis=0)
    att = _attention(qkv.reshape(B, S, -1), lam_vecs, attn_norm_g)

    x1, h2_t, qp_t = _mix(x2, att.reshape(T, ATT_WIDTH), cbh, conv_w[0], conv_norm_g,
                          w_o[0].astype(BF16), mod3, norm2_g, peer_wq[0].T.astype(BF16), S)

    keys = peer_keys[0].reshape(2 * PEER_HEADS, N_KEYS, -1).astype(BF16)
    n1, e1, rf = _select(qp_t, keys)

    out = _experts(h2_t, peer_u[0].astype(BF16), peer_v[0].T.astype(BF16),
                   n1, e1, rf, x1, mod3, final_g.reshape(1, D), S)
    return out.reshape(B, S, D)
```

```python
import functools
import math

import jax
import jax.numpy as jnp
from jax import lax
from jax.experimental import pallas as pl
from jax.experimental.pallas import tpu as pltpu

F32 = jnp.float32
BF16 = jnp.bfloat16

EPS = 1e-6
CHUNK = 64
ATT_HEADS = 4
ATT_HEAD_DIM = 64
ATT_V_DIM = 128
ATT_WIDTH = ATT_HEADS * ATT_V_DIM
CONV_K = 3
PEER_HEADS = 8
N_KEYS = 128
PEER_TOPK = 16
LAM_INIT = 0.8 - 0.6 * math.exp(0.0)
INV_SQRT2 = 0.7071067811865476

LANES = 128
SUBLANES = 8
BF16_ROWS = 2 * SUBLANES
Q_BLK = 256
TOK_BLK = 512
EXPERT_BLK = 2048
EXPERT_GRP = 256
VMEM_LIMIT = 48 * 1024 * 1024


def _cparams(sem):
    return pltpu.CompilerParams(dimension_semantics=sem, vmem_limit_bytes=VMEM_LIMIT)


def _rms(x):
    return x * lax.rsqrt(jnp.mean(x * x, axis=-1, keepdims=True) + EPS)


def _mod_kernel(c_ref, w_ref, b_ref, o_ref):
    sc = jax.nn.silu(c_ref[...])
    o_ref[...] = jnp.dot(sc.astype(BF16), w_ref[...].astype(BF16),
                         preferred_element_type=F32) + b_ref[...]


def _modulation(c, ada_w, ada_b):
    B, D = c.shape
    N = ada_w.shape[1]
    return pl.pallas_call(
        _mod_kernel,
        grid=(N // D,),
        in_specs=[pl.BlockSpec((B, D), lambda j: (0, 0)),
                  pl.BlockSpec((D, D), lambda j: (0, j)),
                  pl.BlockSpec((1, D), lambda j: (0, j))],
        out_specs=pl.BlockSpec((B, D), lambda j: (0, j)),
        out_shape=jax.ShapeDtypeStruct((B, N), F32),
        compiler_params=_cparams(("arbitrary",)),
        name="mod",
    )(c, ada_w, ada_b)


def _inproj_kernel(x_ref, mod_ref, g_ref, w_ref, qkv_ref, cbh_ref):
    y = _rms(x_ref[...])
    h = (y * g_ref[...]) * (1.0 + mod_ref[1:2, :]) + mod_ref[0:1, :]
    p = jnp.dot(h.astype(BF16), w_ref[...], preferred_element_type=F32)
    n_att = qkv_ref.shape[1]
    qkv_ref[...] = p[:, :n_att].astype(BF16)
    cbh_ref[...] = p[:, n_att:]


def _inproj(x2, mod3, g, w, S):
    T, D = x2.shape
    n_att = 3 * ATT_WIDTH
    n_conv = w.shape[1] - n_att
    per_b = S // TOK_BLK
    return pl.pallas_call(
        _inproj_kernel,
        grid=(T // TOK_BLK,),
        in_specs=[pl.BlockSpec((TOK_BLK, D), lambda i: (i, 0)),
                  pl.BlockSpec((None, 6, D), lambda i: (i // per_b, 0, 0)),
                  pl.BlockSpec((1, D), lambda i: (0, 0)),
                  pl.BlockSpec(w.shape, lambda i: (0, 0))],
        out_specs=[pl.BlockSpec((TOK_BLK, n_att), lambda i: (i, 0)),
                   pl.BlockSpec((TOK_BLK, n_conv), lambda i: (i, 0))],
        out_shape=[jax.ShapeDtypeStruct((T, n_att), BF16),
                   jax.ShapeDtypeStruct((T, n_conv), F32)],
        compiler_params=_cparams(("arbitrary",)),
        name="inproj",
    )(x2, mod3, g, w)


def _attn_kernel(q_ref, k_ref, v_ref, lam_ref, g_ref, o_ref, bias_ref, vext_ref):
    S = q_ref.shape[0]
    h = pl.program_id(1)
    vext_ref[:, :ATT_V_DIM] = v_ref[...]
    vext_ref[:, ATT_V_DIM:] = jnp.ones((S, ATT_V_DIM), BF16)

    @pl.when(pl.program_id(0) == 0)
    def _():
        slope = jnp.where(h == 0, 2.0 ** -2,
                          jnp.where(h == 1, 2.0 ** -4,
                                    jnp.where(h == 2, 2.0 ** -6, 2.0 ** -8))).astype(F32)
        r = lax.broadcasted_iota(jnp.int32, (Q_BLK, S), 0) + (S - Q_BLK)
        x = lax.broadcasted_iota(jnp.int32, (Q_BLK, S), 1)
        bias = -slope * jnp.abs(r - x).astype(F32)
        allowed = (x // CHUNK) <= (r // CHUNK)
        bias_ref[h] = jnp.where(allowed, bias, -jnp.inf)

    lv = lam_ref[...]
    lam = (jnp.exp(jnp.sum(lv[0:1, :] * lv[1:2, :], axis=-1, keepdims=True))
           - jnp.exp(jnp.sum(lv[2:3, :] * lv[3:4, :], axis=-1, keepdims=True))
           + LAM_INIT)
    gain = g_ref[...] * (1.0 - LAM_INIT)
    nt = (((1,), (1,)), ((), ()))
    for j in range(S // Q_BLK):
        q0 = j * Q_BLK
        L = q0 + Q_BLK
        qb = q_ref[q0:q0 + Q_BLK, :]
        lane = lax.broadcasted_iota(jnp.int32, qb.shape, 1)
        scale = jnp.asarray(ATT_HEAD_DIM ** -0.5, BF16)
        zero = jnp.zeros_like(qb)
        q_lo = jnp.where(lane < ATT_HEAD_DIM, qb, zero) * scale
        q_hi = jnp.where(lane >= ATT_HEAD_DIM, qb, zero) * scale
        kk = k_ref[0:L, :]
        vv = vext_ref[0:L, :]

        def half(qh):
            s = lax.dot_general(qh, kk, nt, preferred_element_type=F32) + bias_ref[h, :, S - L:S]
            m = jnp.max(s, axis=-1, keepdims=True)
            p = jnp.exp(s - m)
            o = jnp.dot(p.astype(BF16), vv, preferred_element_type=F32)
            return o[:, :ATT_V_DIM] / o[:, ATT_V_DIM:ATT_V_DIM + 1]

        o = half(q_lo) - lam * half(q_hi)
        o_ref[q0:q0 + Q_BLK, :] = (_rms(o) * gain).astype(o_ref.dtype)


def _attention(qkv3, lam_vecs, attn_g):
    B, S, _ = qkv3.shape
    dv = ATT_V_DIM
    blk = lambda off: pl.BlockSpec((None, S, dv), lambda b, h: (b, 0, off + h))
    return pl.pallas_call(
        _attn_kernel,
        grid=(B, ATT_HEADS),
        in_specs=[blk(0), blk(ATT_HEADS), blk(2 * ATT_HEADS),
                  pl.BlockSpec(lam_vecs.shape, lambda b, h: (0, 0)),
                  pl.BlockSpec((1, dv), lambda b, h: (0, h))],
        out_specs=pl.BlockSpec((None, S, dv), lambda b, h: (b, 0, h)),
        out_shape=jax.ShapeDtypeStruct((B, S, ATT_WIDTH), BF16),
        scratch_shapes=[pltpu.VMEM((ATT_HEADS, Q_BLK, S), F32),
                        pltpu.VMEM((S, 2 * ATT_V_DIM), BF16)],
        compiler_params=_cparams(("arbitrary", "arbitrary")),
        name="attn",
    )(qkv3, qkv3, qkv3, lam_vecs, attn_g)


def _mix_kernel(x_ref, att_ref, cb_ref, cc_ref, ch_ref, ccp_ref, chp_ref, cw_ref, cg_ref,
                wo_ref, mod_ref, g2_ref, wqt_ref, x1_ref, h2t_ref, qpt_ref, *, blocks_per_seq):
    i = pl.program_id(0)
    u = cc_ref[...] * ch_ref[...]
    first = (i % blocks_per_seq) == 0
    up = jnp.where(first, 0.0, ccp_ref[...] * chp_ref[...])
    row = lax.broadcasted_iota(jnp.int32, up.shape, 0)

    def shifted(k):
        uk = pltpu.roll(u, k, 0)
        top = jnp.where(row < k, pltpu.roll(up, k, 0), uk[0:SUBLANES, :])
        return jnp.concatenate([top, uk[SUBLANES:, :]], axis=0)

    cw = cw_ref[...]
    y = cw[2:3, :] * u + cw[1:2, :] * shifted(1) + cw[0:1, :] * shifted(2)
    cv = _rms(cb_ref[...] * y) * cg_ref[...]
    both = jnp.concatenate([att_ref[...], cv.astype(BF16)], axis=-1)
    mix = jnp.dot(both, wo_ref[...], preferred_element_type=F32)
    x1 = x_ref[...] + mod_ref[2:3, :] * mix
    x1_ref[...] = x1
    h2 = (_rms(x1) * g2_ref[...]) * (1.0 + mod_ref[4:5, :]) + mod_ref[3:4, :]
    h2t = h2.T.astype(BF16)
    h2t_ref[...] = h2t
    qpt_ref[...] = jnp.dot(wqt_ref[...], h2t, preferred_element_type=F32).astype(BF16)


def _mix(x2, att2, cbh, conv_w, conv_g, w_o, mod3, g2, wq_t, S):
    T, D = x2.shape
    C = conv_w.shape[1]
    NQ = wq_t.shape[0]
    per_b = S // TOK_BLK
    rows8 = TOK_BLK // SUBLANES
    prev = lambda col: pl.BlockSpec((SUBLANES, C), lambda i: (jnp.maximum(i * rows8 - 1, 0), col))
    full = lambda a: pl.BlockSpec(a.shape, lambda i: (0,) * a.ndim)
    return pl.pallas_call(
        functools.partial(_mix_kernel, blocks_per_seq=per_b),
        grid=(T // TOK_BLK,),
        in_specs=[pl.BlockSpec((TOK_BLK, D), lambda i: (i, 0)),
                  pl.BlockSpec((TOK_BLK, ATT_WIDTH), lambda i: (i, 0)),
                  pl.BlockSpec((TOK_BLK, C), lambda i: (i, 0)),
                  pl.BlockSpec((TOK_BLK, C), lambda i: (i, 1)),
                  pl.BlockSpec((TOK_BLK, C), lambda i: (i, 2)),
                  prev(1), prev(2),
                  full(conv_w), full(conv_g), full(w_o),
                  pl.BlockSpec((None, 6, D), lambda i: (i // per_b, 0, 0)),
                  full(g2), full(wq_t)],
        out_specs=[pl.BlockSpec((TOK_BLK, D), lambda i: (i, 0)),
                   pl.BlockSpec((D, TOK_BLK), lambda i: (0, i)),
                   pl.BlockSpec((NQ, TOK_BLK), lambda i: (0, i))],
        out_shape=[jax.ShapeDtypeStruct((T, D), F32),
                   jax.ShapeDtypeStruct((D, T), BF16),
                   jax.ShapeDtypeStruct((NQ, T), BF16)],
        compiler_params=_cparams(("arbitrary",)),
        name="mix",
    )(x2, att2, cbh, cbh, cbh, cbh, cbh, conv_w, conv_g, w_o, mod3, g2, wq_t)


_PAIR_ROWS = ((0, 8, 0), (0, 8, 8), (1, 8, 0), (2, 5, 0), (3, 4, 0), (4, 3, 0),
              (5, 2, 0), (6, 2, 0), (7, 2, 0))


def _batcher_pairs(n):
    pairs = []
    p = 1
    while p < n:
        k = p
        while k >= 1:
            for j in range(k % p, n - k, 2 * k):
                for i in range(min(k, n - j - k)):
                    if (i + j) // (2 * p) == (i + j + k) // (2 * p):
                        pairs.append((i + j, i + j + k))
            k //= 2
        p *= 2
    return pairs


def _sorted_top16(x):
    v = [x[i * SUBLANES:(i + 1) * SUBLANES, :] for i in range(PEER_TOPK)]

    def exchange(i, j):
        v[i], v[j] = jnp.maximum(v[i], v[j]), jnp.minimum(v[i], v[j])

    for i, j in _batcher_pairs(PEER_TOPK):
        exchange(i, j)
    shift = SUBLANES // 2
    while shift >= 1:
        other = [pltpu.roll(t, shift, 0) for t in v]
        v = [jnp.maximum(v[i], other[PEER_TOPK - 1 - i]) for i in range(PEER_TOPK)]
        d = PEER_TOPK // 2
        while d >= 1:
            for i in range(PEER_TOPK):
                if i & d == 0:
                    exchange(i, i + d)
            d //= 2
        shift //= 2
    return v


def _select_kernel(qpt_ref, keys_ref, n1_ref, e1_ref, rf_ref, sc_ref, a_ref, b_ref):
    TB = qpt_ref.shape[1]
    for hp in range(2 * PEER_HEADS):
        sc_ref[hp] = jnp.dot(keys_ref[hp], qpt_ref[hp * N_KEYS:(hp + 1) * N_KEYS, :],
                             preferred_element_type=F32)

    row8 = lax.broadcasted_iota(jnp.int32, (SUBLANES, LANES), 0)
    row16 = lax.broadcasted_iota(jnp.int32, (PEER_TOPK, LANES), 0)

    def per_head(h, carry):
        for lt in range(TB // LANES):
            lanes = slice(lt * LANES, (lt + 1) * LANES)
            s1 = sc_ref[2 * h, :, lanes]
            s2 = sc_ref[2 * h + 1, :, lanes]
            top1 = _sorted_top16(s1)
            top2 = _sorted_top16(s2)
            for r in range(PEER_TOPK):
                a_ref[r:r + 1, :] = top1[r][0:1, :]
                b_ref[r:r + 1, :] = top2[r][0:1, :]
            chunks = [s2[i * SUBLANES:(i + 1) * SUBLANES, :] for i in range(N_KEYS // SUBLANES)]
            ranks = [jnp.zeros_like(ch) for ch in chunks]
            for r in range(PEER_TOPK):
                ranks = [jnp.where(top2[r] > ch, float(r + 1), rk) for ch, rk in zip(chunks, ranks)]
            rank2 = jnp.concatenate(ranks, axis=0)
            a = a_ref[...]
            b = b_ref[...]
            ea = jnp.exp(a - a[0:1, :])
            eb = jnp.exp(b - b[0:1, :])
            cands, prods = [], []
            for (i, n, j0) in _PAIR_ROWS:
                cs = a[i:i + 1, :] + b[j0:j0 + SUBLANES, :]
                ps = ea[i:i + 1, :] * eb[j0:j0 + SUBLANES, :]
                if n < SUBLANES:
                    cs = jnp.where(row8 < n, cs, -jnp.inf)
                cands.append(cs)
                prods.append(ps)
            cands.append(a[SUBLANES:, :] + b[0:1, :])
            prods.append(ea[SUBLANES:, :] * eb[0:1, :])
            cand = jnp.concatenate(cands, axis=0)
            prod = jnp.concatenate(prods, axis=0)
            pad = jnp.full((PEER_TOPK * SUBLANES - cand.shape[0], LANES), -jnp.inf, F32)
            tau = _sorted_top16(jnp.concatenate([cand, pad], axis=0))[PEER_TOPK - 1][0:1, :]
            z = jnp.sum(jnp.where(cand >= tau, prod, 0.0), axis=0, keepdims=True)
            n1 = jnp.zeros_like(s1)
            for j in range(4):
                n1 = n1 + jnp.where(s1 + b[j:j + 1, :] >= tau, 1.0, 0.0)
            extra = jnp.zeros_like(s1)
            for k in reversed(range(3)):
                beyond = jnp.where((a[k:k + 1, :] + b >= tau) & (row16 >= 4), 1.0, 0.0)
                extra = jnp.where(s1 == a[k:k + 1, :], jnp.sum(beyond, axis=0, keepdims=True), extra)
            n1 = n1 + extra
            n1_ref[h, :, lanes] = n1
            e1_ref[h, :, lanes] = 0.5 * jnp.exp(s1 - a[0:1, :])
            r2 = pltpu.bitcast(rank2.astype(BF16), jnp.int32)
            f2 = pltpu.bitcast((jnp.exp(s2 - b[0:1, :]) / z).astype(BF16), jnp.int32)
            for k in range(N_KEYS // BF16_ROWS):
                ks = slice(k * SUBLANES, (k + 1) * SUBLANES)
                rf_ref[h, lt, 2 * k] = r2[ks, :]
                rf_ref[h, lt, 2 * k + 1] = f2[ks, :]
        return carry

    lax.fori_loop(0, PEER_HEADS, per_head, 0)


def _select(qp_t, keys):
    NQ, T = qp_t.shape
    H = PEER_HEADS
    rows = pl.BlockSpec((H, N_KEYS, TOK_BLK), lambda i: (0, 0, i))
    rows_shape = jax.ShapeDtypeStruct((H, N_KEYS, T), F32)
    n_vregs = 2 * (N_KEYS // BF16_ROWS)
    tiles = pl.BlockSpec((H, TOK_BLK // LANES, n_vregs, SUBLANES, LANES), lambda i: (0, i, 0, 0, 0))
    tiles_shape = jax.ShapeDtypeStruct((H, T // LANES, n_vregs, SUBLANES, LANES), jnp.int32)
    return pl.pallas_call(
        _select_kernel,
        grid=(T // TOK_BLK,),
        in_specs=[pl.BlockSpec((NQ, TOK_BLK), lambda i: (0, i)),
                  pl.BlockSpec(keys.shape, lambda i: (0, 0, 0))],
        out_specs=[rows, rows, tiles],
        out_shape=[rows_shape, rows_shape, tiles_shape],
        scratch_shapes=[pltpu.VMEM((2 * H, N_KEYS, TOK_BLK), F32),
                        pltpu.VMEM((PEER_TOPK, LANES), F32),
                        pltpu.VMEM((PEER_TOPK, LANES), F32)],
        compiler_params=_cparams(("arbitrary",)),
        name="select",
    )(qp_t, keys)


def _expert_kernel(h2t_ref, u_ref, vt_ref, n1_ref, e1_ref, rf_ref,
                   x1_ref, mod_ref, fg_ref, o_ref, acc_ref, a_ref, p_ref, *w_refs):
    c = pl.program_id(1)
    EC, TB = a_ref.shape

    @pl.when(c == 0)
    def _():
        acc_ref[...] = jnp.zeros_like(acc_ref)

    groups = [slice(g * EXPERT_GRP, (g + 1) * EXPERT_GRP) for g in range(EC // EXPERT_GRP)]
    for grp in groups:
        words = slice(grp.start // 2, grp.stop // 2)
        a_ref[grp, :] = jnp.dot(pltpu.bitcast(u_ref[words, :], BF16), h2t_ref[...],
                                preferred_element_type=F32)

    zero = jnp.minimum(pl.program_id(1), 0)
    keys_per_grp = EXPERT_GRP // N_KEYS
    tiles_per_key = N_KEYS // BF16_ROWS

    def w_rows(jj, k):
        return pl.ds(pl.multiple_of(zero + (jj * N_KEYS + k * BF16_ROWS), BF16_ROWS), BF16_ROWS)

    for g, w_ref in enumerate(w_refs):
        for jj in range(keys_per_grp):
            j = g * keys_per_grp + jj
            for lt in range(TB // LANES):
                lanes = slice(lt * LANES, (lt + 1) * LANES)
                w = [jnp.zeros((BF16_ROWS, LANES), BF16) for _ in range(tiles_per_key)]
                for h in range(PEER_HEADS):
                    n1 = jnp.broadcast_to(n1_ref[h, j:j + 1, lanes], (BF16_ROWS, LANES)).astype(BF16)
                    e1 = jnp.broadcast_to(e1_ref[h, j:j + 1, lanes], (BF16_ROWS, LANES)).astype(BF16)
                    for k in range(tiles_per_key):
                        hit = pltpu.bitcast(rf_ref[h, lt, 2 * k], BF16) < n1
                        f2 = pltpu.bitcast(rf_ref[h, lt, 2 * k + 1], BF16)
                        w[k] = w[k] + jnp.where(hit, f2 * e1, jnp.zeros_like(e1))
                for k in range(tiles_per_key):
                    w_ref[w_rows(jj, k), lanes] = w[k]

    for g, w_ref in enumerate(w_refs):
        for jj in range(keys_per_grp):
            j = g * keys_per_grp + jj
            for lt in range(TB // LANES):
                lanes = slice(lt * LANES, (lt + 1) * LANES)
                for k in range(tiles_per_key):
                    rows = slice(j * N_KEYS + k * BF16_ROWS, j * N_KEYS + (k + 1) * BF16_ROWS)
                    a = a_ref[rows, lanes]
                    gelu2 = a * (1.0 + lax.erf(a * INV_SQRT2))
                    p_ref[rows, lanes] = w_ref[w_rows(jj, k), lanes] * gelu2.astype(BF16)

    update = None
    for grp in groups:
        part = jnp.dot(pltpu.bitcast(vt_ref[:, grp], BF16), p_ref[grp, :], preferred_element_type=F32)
        update = part if update is None else update + part
    acc_ref[...] += update

    @pl.when(c == pl.num_programs(1) - 1)
    def _():
        x2 = x1_ref[...] + mod_ref[5:6, :] * acc_ref[...].T
        o_ref[...] = _rms(x2) * fg_ref[...]


def _pack_rows_kernel(x_ref, o_ref):
    o_ref[...] = pltpu.bitcast(x_ref[...].astype(BF16), jnp.int32)


def _pack_transposed_kernel(x_ref, o_ref):
    o_ref[...] = pltpu.bitcast(x_ref[...].T.astype(BF16), jnp.int32)


def _pack_expert_tables(u, v):
    E, D = u.shape
    u_words = pl.pallas_call(
        _pack_rows_kernel,
        grid=(E // TOK_BLK,),
        in_specs=[pl.BlockSpec((TOK_BLK, D), lambda g: (g, 0))],
        out_specs=pl.BlockSpec((TOK_BLK // 2, D), lambda g: (g, 0)),
        out_shape=jax.ShapeDtypeStruct((E // 2, D), jnp.int32),
        compiler_params=_cparams(("arbitrary",)),
        name="pack_u",
    )(u)
    vt_words = pl.pallas_call(
        _pack_transposed_kernel,
        grid=(E // TOK_BLK,),
        in_specs=[pl.BlockSpec((TOK_BLK, D), lambda g: (g, 0))],
        out_specs=pl.BlockSpec((D // 2, TOK_BLK), lambda g: (0, g)),
        out_shape=jax.ShapeDtypeStruct((D // 2, E), jnp.int32),
        compiler_params=_cparams(("arbitrary",)),
        name="pack_vt",
    )(v)
    return u_words, vt_words


def _experts(h2_t, u_words, vt_words, n1, e1, rf, x1, mod3, final_g, S):
    D, T = h2_t.shape
    E = vt_words.shape[1]
    H = PEER_HEADS
    TB, EC = TOK_BLK, EXPERT_BLK
    per_b = S // TB
    rows = pl.BlockSpec((H, EC // N_KEYS, TB), lambda i, c: (0, c, i))
    tiles = pl.BlockSpec((H, TB // LANES) + rf.shape[2:], lambda i, c: (0, i, 0, 0, 0))
    return pl.pallas_call(
        _expert_kernel,
        grid=(T // TB, E // EC),
        in_specs=[pl.BlockSpec((D, TB), lambda i, c: (0, i)),
                  pl.BlockSpec((EC // 2, D), lambda i, c: (c, 0)),
                  pl.BlockSpec((D // 2, EC), lambda i, c: (0, c)),
                  rows, rows, tiles,
                  pl.BlockSpec((TB, D), lambda i, c: (i, 0)),
                  pl.BlockSpec((None, 6, D), lambda i, c: (i // per_b, 0, 0)),
                  pl.BlockSpec((1, D), lambda i, c: (0, 0))],
        out_specs=pl.BlockSpec((TB, D), lambda i, c: (i, 0)),
        out_shape=jax.ShapeDtypeStruct((T, D), F32),
        scratch_shapes=[pltpu.VMEM((D, TB), F32),
                        pltpu.VMEM((EC, TB), F32),
                        pltpu.VMEM((EC, TB), BF16)]
                       + [pltpu.VMEM((EXPERT_GRP, TB), BF16)] * (EC // EXPERT_GRP),
        compiler_params=_cparams(("arbitrary", "arbitrary")),
        name="experts",
    )(h2_t, u_words, vt_words, n1, e1, rf, x1, mod3, final_g)


def kernel(x, c, ada_w, ada_b, norm1_g, w_in, conv_w, lam_q1, lam_k1, lam_q2, lam_k2, attn_norm_g, conv_norm_g, w_o, norm2_g, peer_wq, peer_keys, peer_u, peer_v, final_g):
    B, S, D = x.shape
    T = B * S
    assert ada_w.shape[0] == 1, "single-layer block"
    assert S % TOK_BLK == 0 and S % Q_BLK == 0 and Q_BLK % CHUNK == 0
    assert peer_u.shape[1] % EXPERT_BLK == 0 and (EXPERT_BLK // N_KEYS) % SUBLANES == 0

    x2 = x.reshape(T, D)
    mod3 = _modulation(c, ada_w[0], ada_b).reshape(B, 6, D)

    qkv, cbh = _inproj(x2, mod3, norm1_g, w_in[0].astype(BF16), S)

    lam_vecs = jnp.concatenate([lam_q1, lam_k1, lam_q2, lam_k2], axis=0)
    att = _attention(qkv.reshape(B, S, -1), lam_vecs, attn_norm_g)

    x1, h2_t, qp_t = _mix(x2, att.reshape(T, ATT_WIDTH), cbh, conv_w[0], conv_norm_g,
                          w_o[0].astype(BF16), mod3, norm2_g, peer_wq[0].T.astype(BF16), S)

    keys = peer_keys[0].reshape(2 * PEER_HEADS, N_KEYS, -1).astype(BF16)
    n1, e1, rf = _select(qp_t, keys)

    u_words, vt_words = _pack_expert_tables(peer_u[0], peer_v[0])
    out = _experts(h2_t, u_words, vt_words, n1, e1, rf, x1, mod3, final_g.reshape(1, D), S)
    return out.reshape(B, S, D)
```

```python
import functools
import math

import jax
import jax.numpy as jnp
from jax import lax
from jax.experimental import pallas as pl
from jax.experimental.pallas import tpu as pltpu

F32 = jnp.float32
BF16 = jnp.bfloat16

EPS = 1e-6
CHUNK = 64
ATT_HEADS = 4
ATT_HEAD_DIM = 64
ATT_V_DIM = 128
ATT_WIDTH = ATT_HEADS * ATT_V_DIM
CONV_K = 3
PEER_HEADS = 8
N_KEYS = 128
PEER_TOPK = 16
LAM_INIT = 0.8 - 0.6 * math.exp(0.0)
INV_SQRT2 = 0.7071067811865476

LANES = 128
SUBLANES = 8
BF16_ROWS = 2 * SUBLANES
Q_BLK = 256
TOK_BLK = 512
EXPERT_BLK = 2048
EXPERT_GRP = 256
VMEM_LIMIT = 48 * 1024 * 1024


def _cparams(sem):
    return pltpu.CompilerParams(dimension_semantics=sem, vmem_limit_bytes=VMEM_LIMIT)


def _rms(x):
    return x * lax.rsqrt(jnp.mean(x * x, axis=-1, keepdims=True) + EPS)


def _mod_kernel(c_ref, w_ref, b_ref, o_ref):
    sc = jax.nn.silu(c_ref[...])
    o_ref[...] = jnp.dot(sc.astype(BF16), w_ref[...].astype(BF16),
                         preferred_element_type=F32) + b_ref[...]


def _modulation(c, ada_w, ada_b):
    B, D = c.shape
    N = ada_w.shape[1]
    return pl.pallas_call(
        _mod_kernel,
        grid=(N // D,),
        in_specs=[pl.BlockSpec((B, D), lambda j: (0, 0)),
                  pl.BlockSpec((D, D), lambda j: (0, j)),
                  pl.BlockSpec((1, D), lambda j: (0, j))],
        out_specs=pl.BlockSpec((B, D), lambda j: (0, j)),
        out_shape=jax.ShapeDtypeStruct((B, N), F32),
        compiler_params=_cparams(("arbitrary",)),
        name="mod",
    )(c, ada_w, ada_b)


def _inproj_kernel(x_ref, mod_ref, g_ref, w_ref, qkv_ref, cbh_ref):
    y = _rms(x_ref[...])
    h = (y * g_ref[...]) * (1.0 + mod_ref[1:2, :]) + mod_ref[0:1, :]
    p = jnp.dot(h.astype(BF16), w_ref[...], preferred_element_type=F32)
    n_att = qkv_ref.shape[1]
    qkv_ref[...] = p[:, :n_att].astype(BF16)
    cbh_ref[...] = p[:, n_att:]


def _inproj(x2, mod3, g, w, S):
    T, D = x2.shape
    n_att = 3 * ATT_WIDTH
    n_conv = w.shape[1] - n_att
    per_b = S // TOK_BLK
    return pl.pallas_call(
        _inproj_kernel,
        grid=(T // TOK_BLK,),
        in_specs=[pl.BlockSpec((TOK_BLK, D), lambda i: (i, 0)),
                  pl.BlockSpec((None, 6, D), lambda i: (i // per_b, 0, 0)),
                  pl.BlockSpec((1, D), lambda i: (0, 0)),
                  pl.BlockSpec(w.shape, lambda i: (0, 0))],
        out_specs=[pl.BlockSpec((TOK_BLK, n_att), lambda i: (i, 0)),
                   pl.BlockSpec((TOK_BLK, n_conv), lambda i: (i, 0))],
        out_shape=[jax.ShapeDtypeStruct((T, n_att), BF16),
                   jax.ShapeDtypeStruct((T, n_conv), F32)],
        compiler_params=_cparams(("arbitrary",)),
        name="inproj",
    )(x2, mod3, g, w)


def _attn_kernel(q_ref, k_ref, v_ref, lam_ref, g_ref, o_ref, bias_ref, vext_ref):
    S = q_ref.shape[0]
    h = pl.program_id(1)
    vext_ref[:, :ATT_V_DIM] = v_ref[...]
    vext_ref[:, ATT_V_DIM:] = jnp.ones((S, ATT_V_DIM), BF16)

    @pl.when(pl.program_id(0) == 0)
    def _():
        slope = jnp.where(h == 0, 2.0 ** -2,
                          jnp.where(h == 1, 2.0 ** -4,
                                    jnp.where(h == 2, 2.0 ** -6, 2.0 ** -8))).astype(F32)
        r = lax.broadcasted_iota(jnp.int32, (Q_BLK, S), 0) + (S - Q_BLK)
        x = lax.broadcasted_iota(jnp.int32, (Q_BLK, S), 1)
        bias = -slope * jnp.abs(r - x).astype(F32)
        allowed = (x // CHUNK) <= (r // CHUNK)
        bias_ref[h] = jnp.where(allowed, bias, -jnp.inf)

    lv = lam_ref[...]
    lam = (jnp.exp(jnp.sum(lv[0:1, :] * lv[1:2, :], axis=-1, keepdims=True))
           - jnp.exp(jnp.sum(lv[2:3, :] * lv[3:4, :], axis=-1, keepdims=True))
           + LAM_INIT)
    gain = g_ref[...] * (1.0 - LAM_INIT)
    nt = (((1,), (1,)), ((), ()))
    for j in range(S // Q_BLK):
        q0 = j * Q_BLK
        L = q0 + Q_BLK
        qb = q_ref[q0:q0 + Q_BLK, :]
        lane = lax.broadcasted_iota(jnp.int32, qb.shape, 1)
        scale = jnp.asarray(ATT_HEAD_DIM ** -0.5, BF16)
        zero = jnp.zeros_like(qb)
        q_lo = jnp.where(lane < ATT_HEAD_DIM, qb, zero) * scale
        q_hi = jnp.where(lane >= ATT_HEAD_DIM, qb, zero) * scale
        kk = k_ref[0:L, :]
        vv = vext_ref[0:L, :]

        def half(qh):
            s = lax.dot_general(qh, kk, nt, preferred_element_type=F32) + bias_ref[h, :, S - L:S]
            m = jnp.max(s, axis=-1, keepdims=True)
            p = jnp.exp(s - m)
            o = jnp.dot(p.astype(BF16), vv, preferred_element_type=F32)
            return o[:, :ATT_V_DIM] / o[:, ATT_V_DIM:ATT_V_DIM + 1]

        o = half(q_lo) - lam * half(q_hi)
        o_ref[q0:q0 + Q_BLK, :] = (_rms(o) * gain).astype(o_ref.dtype)


def _attention(qkv3, lam_vecs, attn_g):
    B, S, _ = qkv3.shape
    dv = ATT_V_DIM
    blk = lambda off: pl.BlockSpec((None, S, dv), lambda b, h: (b, 0, off + h))
    return pl.pallas_call(
        _attn_kernel,
        grid=(B, ATT_HEADS),
        in_specs=[blk(0), blk(ATT_HEADS), blk(2 * ATT_HEADS),
                  pl.BlockSpec(lam_vecs.shape, lambda b, h: (0, 0)),
                  pl.BlockSpec((1, dv), lambda b, h: (0, h))],
        out_specs=pl.BlockSpec((None, S, dv), lambda b, h: (b, 0, h)),
        out_shape=jax.ShapeDtypeStruct((B, S, ATT_WIDTH), BF16),
        scratch_shapes=[pltpu.VMEM((ATT_HEADS, Q_BLK, S), F32),
                        pltpu.VMEM((S, 2 * ATT_V_DIM), BF16)],
        compiler_params=_cparams(("arbitrary", "arbitrary")),
        name="attn",
    )(qkv3, qkv3, qkv3, lam_vecs, attn_g)


def _mix_kernel(x_ref, att_ref, cb_ref, cc_ref, ch_ref, ccp_ref, chp_ref, cw_ref, cg_ref,
                wo_ref, mod_ref, g2_ref, wqt_ref, x1_ref, h2t_ref, qpt_ref, *, blocks_per_seq):
    i = pl.program_id(0)
    u = cc_ref[...] * ch_ref[...]
    first = (i % blocks_per_seq) == 0
    up = jnp.where(first, 0.0, ccp_ref[...] * chp_ref[...])
    row = lax.broadcasted_iota(jnp.int32, up.shape, 0)

    def shifted(k):
        uk = pltpu.roll(u, k, 0)
        top = jnp.where(row < k, pltpu.roll(up, k, 0), uk[0:SUBLANES, :])
        return jnp.concatenate([top, uk[SUBLANES:, :]], axis=0)

    cw = cw_ref[...]
    y = cw[2:3, :] * u + cw[1:2, :] * shifted(1) + cw[0:1, :] * shifted(2)
    cv = _rms(cb_ref[...] * y) * cg_ref[...]
    both = jnp.concatenate([att_ref[...], cv.astype(BF16)], axis=-1)
    mix = jnp.dot(both, wo_ref[...], preferred_element_type=F32)
    x1 = x_ref[...] + mod_ref[2:3, :] * mix
    x1_ref[...] = x1
    h2 = (_rms(x1) * g2_ref[...]) * (1.0 + mod_ref[4:5, :]) + mod_ref[3:4, :]
    h2t = h2.T.astype(BF16)
    h2t_ref[...] = h2t
    qpt_ref[...] = jnp.dot(wqt_ref[...], h2t, preferred_element_type=F32).astype(BF16)


def _mix(x2, att2, cbh, conv_w, conv_g, w_o, mod3, g2, wq_t, S):
    T, D = x2.shape
    C = conv_w.shape[1]
    NQ = wq_t.shape[0]
    per_b = S // TOK_BLK
    rows8 = TOK_BLK // SUBLANES
    prev = lambda col: pl.BlockSpec((SUBLANES, C), lambda i: (jnp.maximum(i * rows8 - 1, 0), col))
    full = lambda a: pl.BlockSpec(a.shape, lambda i: (0,) * a.ndim)
    return pl.pallas_call(
        functools.partial(_mix_kernel, blocks_per_seq=per_b),
        grid=(T // TOK_BLK,),
        in_specs=[pl.BlockSpec((TOK_BLK, D), lambda i: (i, 0)),
                  pl.BlockSpec((TOK_BLK, ATT_WIDTH), lambda i: (i, 0)),
                  pl.BlockSpec((TOK_BLK, C), lambda i: (i, 0)),
                  pl.BlockSpec((TOK_BLK, C), lambda i: (i, 1)),
                  pl.BlockSpec((TOK_BLK, C), lambda i: (i, 2)),
                  prev(1), prev(2),
                  full(conv_w), full(conv_g), full(w_o),
                  pl.BlockSpec((None, 6, D), lambda i: (i // per_b, 0, 0)),
                  full(g2), full(wq_t)],
        out_specs=[pl.BlockSpec((TOK_BLK, D), lambda i: (i, 0)),
                   pl.BlockSpec((D, TOK_BLK), lambda i: (0, i)),
                   pl.BlockSpec((NQ, TOK_BLK), lambda i: (0, i))],
        out_shape=[jax.ShapeDtypeStruct((T, D), F32),
                   jax.ShapeDtypeStruct((D, T), BF16),
                   jax.ShapeDtypeStruct((NQ, T), BF16)],
        compiler_params=_cparams(("arbitrary",)),
        name="mix",
    )(x2, att2, cbh, cbh, cbh, cbh, cbh, conv_w, conv_g, w_o, mod3, g2, wq_t)


_PAIR_ROWS = ((0, 8, 0), (0, 8, 8), (1, 8, 0), (2, 5, 0), (3, 4, 0), (4, 3, 0),
              (5, 2, 0), (6, 2, 0), (7, 2, 0))


def _batcher_pairs(n):
    pairs = []
    p = 1
    while p < n:
        k = p
        while k >= 1:
            for j in range(k % p, n - k, 2 * k):
                for i in range(min(k, n - j - k)):
                    if (i + j) // (2 * p) == (i + j + k) // (2 * p):
                        pairs.append((i + j, i + j + k))
            k //= 2
        p *= 2
    return pairs


def _sorted_top16(x):
    v = [x[i * SUBLANES:(i + 1) * SUBLANES, :] for i in range(PEER_TOPK)]

    def exchange(i, j):
        v[i], v[j] = jnp.maximum(v[i], v[j]), jnp.minimum(v[i], v[j])

    for i, j in _batcher_pairs(PEER_TOPK):
        exchange(i, j)
    shift = SUBLANES // 2
    while shift >= 1:
        other = [pltpu.roll(t, shift, 0) for t in v]
        v = [jnp.maximum(v[i], other[PEER_TOPK - 1 - i]) for i in range(PEER_TOPK)]
        d = PEER_TOPK // 2
        while d >= 1:
            for i in range(PEER_TOPK):
                if i & d == 0:
                    exchange(i, i + d)
            d //= 2
        shift //= 2
    return v


def _rank_in_sorted(top, x):
    def sel(bits, options):
        if not bits:
            return options[0]
        half = len(options) // 2
        return jnp.where(bits[0], sel(bits[1:], options[half:]), sel(bits[1:], options[:half]))

    c8 = top[7] > x
    c4 = sel([c8], [top[3], top[11]]) > x
    c2 = sel([c8, c4], [top[1], top[5], top[9], top[13]]) > x
    c1 = sel([c8, c4, c2], [top[2 * i] for i in range(8)]) > x
    c16 = top[PEER_TOPK - 1] > x
    rank = jnp.where(c8, 8.0, 0.0)
    for bit, weight in ((c4, 4.0), (c2, 2.0), (c1, 1.0), (c16, 1.0)):
        rank = rank + jnp.where(bit, weight, 0.0)
    return rank


def _select_kernel(qpt_ref, keys_ref, n1_ref, e1_ref, rf_ref, sc_ref, a_ref, b_ref):
    TB = qpt_ref.shape[1]
    for hp in range(2 * PEER_HEADS):
        sc_ref[hp] = jnp.dot(keys_ref[hp], qpt_ref[hp * N_KEYS:(hp + 1) * N_KEYS, :],
                             preferred_element_type=F32)

    row8 = lax.broadcasted_iota(jnp.int32, (SUBLANES, LANES), 0)
    row16 = lax.broadcasted_iota(jnp.int32, (PEER_TOPK, LANES), 0)

    def per_head(h, carry):
        for lt in range(TB // LANES):
            lanes = slice(lt * LANES, (lt + 1) * LANES)
            s1 = sc_ref[2 * h, :, lanes]
            s2 = sc_ref[2 * h + 1, :, lanes]
            top1 = _sorted_top16(s1)
            top2 = _sorted_top16(s2)
            for r in range(PEER_TOPK):
                a_ref[r:r + 1, :] = top1[r][0:1, :]
                b_ref[r:r + 1, :] = top2[r][0:1, :]
            rank2 = jnp.concatenate([_rank_in_sorted(top2, s2[i * SUBLANES:(i + 1) * SUBLANES, :])
                                     for i in range(N_KEYS // SUBLANES)], axis=0)
            a = a_ref[...]
            b = b_ref[...]
            ea = jnp.exp(a - a[0:1, :])
            eb = jnp.exp(b - b[0:1, :])
            cands, prods = [], []
            for (i, n, j0) in _PAIR_ROWS:
                cs = a[i:i + 1, :] + b[j0:j0 + SUBLANES, :]
                ps = ea[i:i + 1, :] * eb[j0:j0 + SUBLANES, :]
                if n < SUBLANES:
                    cs = jnp.where(row8 < n, cs, -jnp.inf)
                cands.append(cs)
                prods.append(ps)
            cands.append(a[SUBLANES:, :] + b[0:1, :])
            prods.append(ea[SUBLANES:, :] * eb[0:1, :])
            cand = jnp.concatenate(cands, axis=0)
            prod = jnp.concatenate(prods, axis=0)
            pad = jnp.full((PEER_TOPK * SUBLANES - cand.shape[0], LANES), -jnp.inf, F32)
            tau = _sorted_top16(jnp.concatenate([cand, pad], axis=0))[PEER_TOPK - 1][0:1, :]
            z = jnp.sum(jnp.where(cand >= tau, prod, 0.0), axis=0, keepdims=True)
            n1 = jnp.zeros_like(s1)
            for j in range(4):
                n1 = n1 + jnp.where(s1 + b[j:j + 1, :] >= tau, 1.0, 0.0)
            extra = jnp.zeros_like(s1)
            for k in reversed(range(3)):
                beyond = jnp.where((a[k:k + 1, :] + b >= tau) & (row16 >= 4), 1.0, 0.0)
                extra = jnp.where(s1 == a[k:k + 1, :], jnp.sum(beyond, axis=0, keepdims=True), extra)
            n1 = n1 + extra
            n1_ref[h, :, lanes] = n1
            e1_ref[h, :, lanes] = 0.5 * jnp.exp(s1 - a[0:1, :])
            r2 = pltpu.bitcast(rank2.astype(BF16), jnp.int32)
            f2 = pltpu.bitcast((jnp.exp(s2 - b[0:1, :]) / z).astype(BF16), jnp.int32)
            for k in range(N_KEYS // BF16_ROWS):
                ks = slice(k * SUBLANES, (k + 1) * SUBLANES)
                rf_ref[h, lt, 2 * k] = r2[ks, :]
                rf_ref[h, lt, 2 * k + 1] = f2[ks, :]
        return carry

    lax.fori_loop(0, PEER_HEADS, per_head, 0)


def _select(qp_t, keys):
    NQ, T = qp_t.shape
    H = PEER_HEADS
    rows = pl.BlockSpec((H, N_KEYS, TOK_BLK), lambda i: (0, 0, i))
    rows_shape = jax.ShapeDtypeStruct((H, N_KEYS, T), F32)
    n_vregs = 2 * (N_KEYS // BF16_ROWS)
    tiles = pl.BlockSpec((H, TOK_BLK // LANES, n_vregs, SUBLANES, LANES), lambda i: (0, i, 0, 0, 0))
    tiles_shape = jax.ShapeDtypeStruct((H, T // LANES, n_vregs, SUBLANES, LANES), jnp.int32)
    return pl.pallas_call(
        _select_kernel,
        grid=(T // TOK_BLK,),
        in_specs=[pl.BlockSpec((NQ, TOK_BLK), lambda i: (0, i)),
                  pl.BlockSpec(keys.shape, lambda i: (0, 0, 0))],
        out_specs=[rows, rows, tiles],
        out_shape=[rows_shape, rows_shape, tiles_shape],
        scratch_shapes=[pltpu.VMEM((2 * H, N_KEYS, TOK_BLK), F32),
                        pltpu.VMEM((PEER_TOPK, LANES), F32),
                        pltpu.VMEM((PEER_TOPK, LANES), F32)],
        compiler_params=_cparams(("arbitrary",)),
        name="select",
    )(qp_t, keys)


def _expert_kernel(h2t_ref, u_ref, vt_ref, n1_ref, e1_ref, rf_ref,
                   x1_ref, mod_ref, fg_ref, o_ref, acc_ref, a_ref, p_ref, *w_refs):
    c = pl.program_id(1)
    EC, TB = a_ref.shape

    @pl.when(c == 0)
    def _():
        acc_ref[...] = jnp.zeros_like(acc_ref)

    groups = [slice(g * EXPERT_GRP, (g + 1) * EXPERT_GRP) for g in range(EC // EXPERT_GRP)]
    for grp in groups:
        words = slice(grp.start // 2, grp.stop // 2)
        a_ref[grp, :] = jnp.dot(pltpu.bitcast(u_ref[words, :], BF16), h2t_ref[...],
                                preferred_element_type=F32)

    zero = jnp.minimum(pl.program_id(1), 0)
    keys_per_grp = EXPERT_GRP // N_KEYS
    tiles_per_key = N_KEYS // BF16_ROWS

    def w_rows(jj, k):
        return pl.ds(pl.multiple_of(zero + (jj * N_KEYS + k * BF16_ROWS), BF16_ROWS), BF16_ROWS)

    for g, w_ref in enumerate(w_refs):
        for jj in range(keys_per_grp):
            j = g * keys_per_grp + jj
            for lt in range(TB // LANES):
                lanes = slice(lt * LANES, (lt + 1) * LANES)
                w = [None] * tiles_per_key
                for h in range(PEER_HEADS):
                    n1 = jnp.broadcast_to(n1_ref[h, j:j + 1, lanes], (BF16_ROWS, LANES)).astype(BF16)
                    e1 = jnp.broadcast_to(e1_ref[h, j:j + 1, lanes], (BF16_ROWS, LANES)).astype(BF16)
                    for k in range(tiles_per_key):
                        hit = pltpu.bitcast(rf_ref[h, lt, 2 * k], BF16) < n1
                        f2 = pltpu.bitcast(rf_ref[h, lt, 2 * k + 1], BF16)
                        term = jnp.where(hit, f2 * e1, jnp.zeros_like(e1))
                        w[k] = term if w[k] is None else w[k] + term
                for k in range(tiles_per_key):
                    w_ref[w_rows(jj, k), lanes] = w[k]

    for g, w_ref in enumerate(w_refs):
        for jj in range(keys_per_grp):
            j = g * keys_per_grp + jj
            for lt in range(TB // LANES):
                lanes = slice(lt * LANES, (lt + 1) * LANES)
                for k in range(tiles_per_key):
                    rows = slice(j * N_KEYS + k * BF16_ROWS, j * N_KEYS + (k + 1) * BF16_ROWS)
                    a = a_ref[rows, lanes]
                    gelu2 = a * (1.0 + lax.erf(a * INV_SQRT2))
                    p_ref[rows, lanes] = w_ref[w_rows(jj, k), lanes] * gelu2.astype(BF16)

    update = None
    for grp in groups:
        part = jnp.dot(pltpu.bitcast(vt_ref[:, grp], BF16), p_ref[grp, :], preferred_element_type=F32)
        update = part if update is None else update + part
    acc_ref[...] += update

    @pl.when(c == pl.num_programs(1) - 1)
    def _():
        x2 = x1_ref[...] + mod_ref[5:6, :] * acc_ref[...].T
        o_ref[...] = _rms(x2) * fg_ref[...]


def _pack_rows_kernel(x_ref, o_ref):
    o_ref[...] = pltpu.bitcast(x_ref[...].astype(BF16), jnp.int32)


def _pack_transposed_kernel(x_ref, o_ref):
    o_ref[...] = pltpu.bitcast(x_ref[...].T.astype(BF16), jnp.int32)


def _pack_expert_tables(u, v):
    E, D = u.shape
    u_words = pl.pallas_call(
        _pack_rows_kernel,
        grid=(E // TOK_BLK,),
        in_specs=[pl.BlockSpec((TOK_BLK, D), lambda g: (g, 0))],
        out_specs=pl.BlockSpec((TOK_BLK // 2, D), lambda g: (g, 0)),
        out_shape=jax.ShapeDtypeStruct((E // 2, D), jnp.int32),
        compiler_params=_cparams(("arbitrary",)),
        name="pack_u",
    )(u)
    vt_words = pl.pallas_call(
        _pack_transposed_kernel,
        grid=(E // TOK_BLK,),
        in_specs=[pl.BlockSpec((TOK_BLK, D), lambda g: (g, 0))],
        out_specs=pl.BlockSpec((D // 2, TOK_BLK), lambda g: (0, g)),
        out_shape=jax.ShapeDtypeStruct((D // 2, E), jnp.int32),
        compiler_params=_cparams(("arbitrary",)),
        name="pack_vt",
    )(v)
    return u_words, vt_words


def _experts(h2_t, u_words, vt_words, n1, e1, rf, x1, mod3, final_g, S):
    D, T = h2_t.shape
    E = vt_words.shape[1]
    H = PEER_HEADS
    TB, EC = TOK_BLK, EXPERT_BLK
    per_b = S // TB
    rows = pl.BlockSpec((H, EC // N_KEYS, TB), lambda i, c: (0, c, i))
    tiles = pl.BlockSpec((H, TB // LANES) + rf.shape[2:], lambda i, c: (0, i, 0, 0, 0))
    return pl.pallas_call(
        _expert_kernel,
        grid=(T // TB, E // EC),
        in_specs=[pl.BlockSpec((D, TB), lambda i, c: (0, i)),
                  pl.BlockSpec((EC // 2, D), lambda i, c: (c, 0)),
                  pl.BlockSpec((D // 2, EC), lambda i, c: (0, c)),
                  rows, rows, tiles,
                  pl.BlockSpec((TB, D), lambda i, c: (i, 0)),
                  pl.BlockSpec((None, 6, D), lambda i, c: (i // per_b, 0, 0)),
                  pl.BlockSpec((1, D), lambda i, c: (0, 0))],
        out_specs=pl.BlockSpec((TB, D), lambda i, c: (i, 0)),
        out_shape=jax.ShapeDtypeStruct((T, D), F32),
        scratch_shapes=[pltpu.VMEM((D, TB), F32),
                        pltpu.VMEM((EC, TB), F32),
                        pltpu.VMEM((EC, TB), BF16)]
                       + [pltpu.VMEM((EXPERT_GRP, TB), BF16)] * (EC // EXPERT_GRP),
        compiler_params=_cparams(("arbitrary", "arbitrary")),
        name="experts",
    )(h2_t, u_words, vt_words, n1, e1, rf, x1, mod3, final_g)


def kernel(x, c, ada_w, ada_b, norm1_g, w_in, conv_w, lam_q1, lam_k1, lam_q2, lam_k2, attn_norm_g, conv_norm_g, w_o, norm2_g, peer_wq, peer_keys, peer_u, peer_v, final_g):
    B, S, D = x.shape
    T = B * S
    assert ada_w.shape[0] == 1, "single-layer block"
    assert S % TOK_BLK == 0 and S % Q_BLK == 0 and Q_BLK % CHUNK == 0
    assert peer_u.shape[1] % EXPERT_BLK == 0 and (EXPERT_BLK // N_KEYS) % SUBLANES == 0

    x2 = x.reshape(T, D)
    mod3 = _modulation(c, ada_w[0], ada_b).reshape(B, 6, D)

    qkv, cbh = _inproj(x2, mod3, norm1_g, w_in[0].astype(BF16), S)

    lam_vecs = jnp.concatenate([lam_q1, lam_k1, lam_q2, lam_k2], axis=0)
    att = _attention(qkv.reshape(B, S, -1), lam_vecs, attn_norm_g)

    x1, h2_t, qp_t = _mix(x2, att.reshape(T, ATT_WIDTH), cbh, conv_w[0], conv_norm_g,
                          w_o[0].astype(BF16), mod3, norm2_g, peer_wq[0].T.astype(BF16), S)

    keys = peer_keys[0].reshape(2 * PEER_HEADS, N_KEYS, -1).astype(BF16)
    n1, e1, rf = _select(qp_t, keys)

    u_words, vt_words = _pack_expert_tables(peer_u[0], peer_v[0])
    out = _experts(h2_t, u_words, vt_words, n1, e1, rf, x1, mod3, final_g.reshape(1, D), S)
    return out.reshape(B, S, D)
```

```python
import functools
import math

import jax
import jax.numpy as jnp
from jax import lax
from jax.experimental import pallas as pl
from jax.experimental.pallas import tpu as pltpu

F32 = jnp.float32
BF16 = jnp.bfloat16

EPS = 1e-6
CHUNK = 64
ATT_HEADS = 4
ATT_HEAD_DIM = 64
ATT_V_DIM = 128
ATT_WIDTH = ATT_HEADS * ATT_V_DIM
CONV_K = 3
PEER_HEADS = 8
N_KEYS = 128
PEER_TOPK = 16
LAM_INIT = 0.8 - 0.6 * math.exp(0.0)
INV_SQRT2 = 0.7071067811865476

LANES = 128
SUBLANES = 8
BF16_ROWS = 2 * SUBLANES
Q_BLK = 256
TOK_BLK = 512
EXPERT_BLK = 2048
EXPERT_GRP = 256
VMEM_LIMIT = 48 * 1024 * 1024


def _cparams(sem):
    return pltpu.CompilerParams(dimension_semantics=sem, vmem_limit_bytes=VMEM_LIMIT)


def _rms(x):
    return x * lax.rsqrt(jnp.mean(x * x, axis=-1, keepdims=True) + EPS)


def _mod_kernel(c_ref, w_ref, b_ref, o_ref):
    sc = jax.nn.silu(c_ref[...])
    o_ref[...] = jnp.dot(sc.astype(BF16), w_ref[...].astype(BF16),
                         preferred_element_type=F32) + b_ref[...]


def _modulation(c, ada_w, ada_b):
    B, D = c.shape
    N = ada_w.shape[1]
    return pl.pallas_call(
        _mod_kernel,
        grid=(N // D,),
        in_specs=[pl.BlockSpec((B, D), lambda j: (0, 0)),
                  pl.BlockSpec((D, D), lambda j: (0, j)),
                  pl.BlockSpec((1, D), lambda j: (0, j))],
        out_specs=pl.BlockSpec((B, D), lambda j: (0, j)),
        out_shape=jax.ShapeDtypeStruct((B, N), F32),
        compiler_params=_cparams(("arbitrary",)),
        name="mod",
    )(c, ada_w, ada_b)


def _inproj_kernel(x_ref, mod_ref, g_ref, w_ref, qkv_ref, cbh_ref):
    y = _rms(x_ref[...])
    h = (y * g_ref[...]) * (1.0 + mod_ref[1:2, :]) + mod_ref[0:1, :]
    p = jnp.dot(h.astype(BF16), w_ref[...], preferred_element_type=F32)
    n_att = qkv_ref.shape[1]
    qkv_ref[...] = p[:, :n_att].astype(BF16)
    cbh_ref[...] = p[:, n_att:]


def _inproj(x2, mod3, g, w, S):
    T, D = x2.shape
    n_att = 3 * ATT_WIDTH
    n_conv = w.shape[1] - n_att
    per_b = S // TOK_BLK
    return pl.pallas_call(
        _inproj_kernel,
        grid=(T // TOK_BLK,),
        in_specs=[pl.BlockSpec((TOK_BLK, D), lambda i: (i, 0)),
                  pl.BlockSpec((None, 6, D), lambda i: (i // per_b, 0, 0)),
                  pl.BlockSpec((1, D), lambda i: (0, 0)),
                  pl.BlockSpec(w.shape, lambda i: (0, 0))],
        out_specs=[pl.BlockSpec((TOK_BLK, n_att), lambda i: (i, 0)),
                   pl.BlockSpec((TOK_BLK, n_conv), lambda i: (i, 0))],
        out_shape=[jax.ShapeDtypeStruct((T, n_att), BF16),
                   jax.ShapeDtypeStruct((T, n_conv), F32)],
        compiler_params=_cparams(("arbitrary",)),
        name="inproj",
    )(x2, mod3, g, w)


def _attn_kernel(q_ref, k_ref, v_ref, lam_ref, g_ref, o_ref, bias_ref, vext_ref):
    S = q_ref.shape[0]
    h = pl.program_id(1)
    vext_ref[:, :ATT_V_DIM] = v_ref[...]
    vext_ref[:, ATT_V_DIM:] = jnp.ones((S, ATT_V_DIM), BF16)

    @pl.when(pl.program_id(0) == 0)
    def _():
        slope = jnp.where(h == 0, 2.0 ** -2,
                          jnp.where(h == 1, 2.0 ** -4,
                                    jnp.where(h == 2, 2.0 ** -6, 2.0 ** -8))).astype(F32)
        r = lax.broadcasted_iota(jnp.int32, (Q_BLK, S), 0) + (S - Q_BLK)
        x = lax.broadcasted_iota(jnp.int32, (Q_BLK, S), 1)
        bias = -slope * jnp.abs(r - x).astype(F32)
        allowed = (x // CHUNK) <= (r // CHUNK)
        bias_ref[h] = jnp.where(allowed, bias, -jnp.inf)

    lv = lam_ref[...]
    lam = (jnp.exp(jnp.sum(lv[0:1, :] * lv[1:2, :], axis=-1, keepdims=True))
           - jnp.exp(jnp.sum(lv[2:3, :] * lv[3:4, :], axis=-1, keepdims=True))
           + LAM_INIT)
    gain = g_ref[...] * (1.0 - LAM_INIT)
    nt = (((1,), (1,)), ((), ()))
    for j in range(S // Q_BLK):
        q0 = j * Q_BLK
        L = q0 + Q_BLK
        qb = q_ref[q0:q0 + Q_BLK, :]
        lane = lax.broadcasted_iota(jnp.int32, qb.shape, 1)
        scale = jnp.asarray(ATT_HEAD_DIM ** -0.5, BF16)
        zero = jnp.zeros_like(qb)
        q_lo = jnp.where(lane < ATT_HEAD_DIM, qb, zero) * scale
        q_hi = jnp.where(lane >= ATT_HEAD_DIM, qb, zero) * scale
        kk = k_ref[0:L, :]
        vv = vext_ref[0:L, :]

        bias = bias_ref[h, :, S - L:S]
        s = lax.dot_general(jnp.concatenate([q_lo, q_hi], axis=0), kk, nt, preferred_element_type=F32)
        s = s + jnp.concatenate([bias, bias], axis=0)
        m = jnp.max(s, axis=-1, keepdims=True)
        p = jnp.exp(s - m)
        ov = jnp.dot(p.astype(BF16), vv, preferred_element_type=F32)
        on = ov[:, :ATT_V_DIM] / ov[:, ATT_V_DIM:ATT_V_DIM + 1]
        o = on[:Q_BLK, :] - lam * on[Q_BLK:, :]
        o_ref[q0:q0 + Q_BLK, :] = (_rms(o) * gain).astype(o_ref.dtype)


def _attention(qkv3, lam_vecs, attn_g):
    B, S, _ = qkv3.shape
    dv = ATT_V_DIM
    blk = lambda off: pl.BlockSpec((None, S, dv), lambda b, h: (b, 0, off + h))
    return pl.pallas_call(
        _attn_kernel,
        grid=(B, ATT_HEADS),
        in_specs=[blk(0), blk(ATT_HEADS), blk(2 * ATT_HEADS),
                  pl.BlockSpec(lam_vecs.shape, lambda b, h: (0, 0)),
                  pl.BlockSpec((1, dv), lambda b, h: (0, h))],
        out_specs=pl.BlockSpec((None, S, dv), lambda b, h: (b, 0, h)),
        out_shape=jax.ShapeDtypeStruct((B, S, ATT_WIDTH), BF16),
        scratch_shapes=[pltpu.VMEM((ATT_HEADS, Q_BLK, S), F32),
                        pltpu.VMEM((S, 2 * ATT_V_DIM), BF16)],
        compiler_params=_cparams(("arbitrary", "arbitrary")),
        name="attn",
    )(qkv3, qkv3, qkv3, lam_vecs, attn_g)


def _mix_kernel(x_ref, att_ref, cb_ref, cc_ref, ch_ref, ccp_ref, chp_ref, cw_ref, cg_ref,
                wo_ref, mod_ref, g2_ref, wqt_ref, x1_ref, h2t_ref, qpt_ref, *, blocks_per_seq):
    i = pl.program_id(0)
    u = cc_ref[...] * ch_ref[...]
    first = (i % blocks_per_seq) == 0
    up = jnp.where(first, 0.0, ccp_ref[...] * chp_ref[...])
    row = lax.broadcasted_iota(jnp.int32, up.shape, 0)

    def shifted(k):
        uk = pltpu.roll(u, k, 0)
        top = jnp.where(row < k, pltpu.roll(up, k, 0), uk[0:SUBLANES, :])
        return jnp.concatenate([top, uk[SUBLANES:, :]], axis=0)

    cw = cw_ref[...]
    y = cw[2:3, :] * u + cw[1:2, :] * shifted(1) + cw[0:1, :] * shifted(2)
    cv = _rms(cb_ref[...] * y) * cg_ref[...]
    both = jnp.concatenate([att_ref[...], cv.astype(BF16)], axis=-1)
    mix = jnp.dot(both, wo_ref[...], preferred_element_type=F32)
    x1 = x_ref[...] + mod_ref[2:3, :] * mix
    x1_ref[...] = x1
    h2 = (_rms(x1) * g2_ref[...]) * (1.0 + mod_ref[4:5, :]) + mod_ref[3:4, :]
    h2t = h2.T.astype(BF16)
    h2t_ref[...] = h2t
    qpt_ref[...] = jnp.dot(wqt_ref[...], h2t, preferred_element_type=F32).astype(BF16)


def _mix(x2, att2, cbh, conv_w, conv_g, w_o, mod3, g2, wq_t, S):
    T, D = x2.shape
    C = conv_w.shape[1]
    NQ = wq_t.shape[0]
    per_b = S // TOK_BLK
    rows8 = TOK_BLK // SUBLANES
    prev = lambda col: pl.BlockSpec((SUBLANES, C), lambda i: (jnp.maximum(i * rows8 - 1, 0), col))
    full = lambda a: pl.BlockSpec(a.shape, lambda i: (0,) * a.ndim)
    return pl.pallas_call(
        functools.partial(_mix_kernel, blocks_per_seq=per_b),
        grid=(T // TOK_BLK,),
        in_specs=[pl.BlockSpec((TOK_BLK, D), lambda i: (i, 0)),
                  pl.BlockSpec((TOK_BLK, ATT_WIDTH), lambda i: (i, 0)),
                  pl.BlockSpec((TOK_BLK, C), lambda i: (i, 0)),
                  pl.BlockSpec((TOK_BLK, C), lambda i: (i, 1)),
                  pl.BlockSpec((TOK_BLK, C), lambda i: (i, 2)),
                  prev(1), prev(2),
                  full(conv_w), full(conv_g), full(w_o),
                  pl.BlockSpec((None, 6, D), lambda i: (i // per_b, 0, 0)),
                  full(g2), full(wq_t)],
        out_specs=[pl.BlockSpec((TOK_BLK, D), lambda i: (i, 0)),
                   pl.BlockSpec((D, TOK_BLK), lambda i: (0, i)),
                   pl.BlockSpec((NQ, TOK_BLK), lambda i: (0, i))],
        out_shape=[jax.ShapeDtypeStruct((T, D), F32),
                   jax.ShapeDtypeStruct((D, T), BF16),
                   jax.ShapeDtypeStruct((NQ, T), BF16)],
        compiler_params=_cparams(("arbitrary",)),
        name="mix",
    )(x2, att2, cbh, cbh, cbh, cbh, cbh, conv_w, conv_g, w_o, mod3, g2, wq_t)


_PAIR_ROWS = ((0, 8, 0), (0, 8, 8), (1, 8, 0), (2, 5, 0), (3, 4, 0), (4, 3, 0),
              (5, 2, 0), (6, 2, 0), (7, 2, 0))


def _batcher_pairs(n):
    pairs = []
    p = 1
    while p < n:
        k = p
        while k >= 1:
            for j in range(k % p, n - k, 2 * k):
                for i in range(min(k, n - j - k)):
                    if (i + j) // (2 * p) == (i + j + k) // (2 * p):
                        pairs.append((i + j, i + j + k))
            k //= 2
        p *= 2
    return pairs


def _sorted_top16(x):
    v = [x[i * SUBLANES:(i + 1) * SUBLANES, :] for i in range(PEER_TOPK)]

    def exchange(i, j):
        v[i], v[j] = jnp.maximum(v[i], v[j]), jnp.minimum(v[i], v[j])

    for i, j in _batcher_pairs(PEER_TOPK):
        exchange(i, j)
    shift = SUBLANES // 2
    while shift >= 1:
        other = [pltpu.roll(t, shift, 0) for t in v]
        v = [jnp.maximum(v[i], other[PEER_TOPK - 1 - i]) for i in range(PEER_TOPK)]
        d = PEER_TOPK // 2
        while d >= 1:
            for i in range(PEER_TOPK):
                if i & d == 0:
                    exchange(i, i + d)
            d //= 2
        shift //= 2
    return v


def _rank_in_sorted(top, x):
    def sel(bits, options):
        if not bits:
            return options[0]
        half = len(options) // 2
        return jnp.where(bits[0], sel(bits[1:], options[half:]), sel(bits[1:], options[:half]))

    c8 = top[7] > x
    c4 = sel([c8], [top[3], top[11]]) > x
    c2 = sel([c8, c4], [top[1], top[5], top[9], top[13]]) > x
    c1 = sel([c8, c4, c2], [top[2 * i] for i in range(8)]) > x
    c16 = top[PEER_TOPK - 1] > x
    rank = jnp.where(c8, 8.0, 0.0)
    for bit, weight in ((c4, 4.0), (c2, 2.0), (c1, 1.0), (c16, 1.0)):
        rank = rank + jnp.where(bit, weight, 0.0)
    return rank


def _select_kernel(qpt_ref, keys_ref, n1_ref, e1_ref, rf_ref, sc_ref, a_ref, b_ref):
    TB = qpt_ref.shape[1]
    for hp in range(2 * PEER_HEADS):
        sc_ref[hp] = jnp.dot(keys_ref[hp], qpt_ref[hp * N_KEYS:(hp + 1) * N_KEYS, :],
                             preferred_element_type=F32)

    row8 = lax.broadcasted_iota(jnp.int32, (SUBLANES, LANES), 0)
    row16 = lax.broadcasted_iota(jnp.int32, (PEER_TOPK, LANES), 0)

    def per_head(h, carry):
        for lt in range(TB // LANES):
            lanes = slice(lt * LANES, (lt + 1) * LANES)
            s1 = sc_ref[2 * h, :, lanes]
            s2 = sc_ref[2 * h + 1, :, lanes]
            top1 = _sorted_top16(s1)
            top2 = _sorted_top16(s2)
            for r in range(PEER_TOPK):
                a_ref[r:r + 1, :] = top1[r][0:1, :]
                b_ref[r:r + 1, :] = top2[r][0:1, :]
            rank2 = jnp.concatenate([_rank_in_sorted(top2, s2[i * SUBLANES:(i + 1) * SUBLANES, :])
                                     for i in range(N_KEYS // SUBLANES)], axis=0)
            a = a_ref[...]
            b = b_ref[...]
            ea = jnp.exp(a - a[0:1, :])
            eb = jnp.exp(b - b[0:1, :])
            cands, prods = [], []
            for (i, n, j0) in _PAIR_ROWS:
                cs = a[i:i + 1, :] + b[j0:j0 + SUBLANES, :]
                ps = ea[i:i + 1, :] * eb[j0:j0 + SUBLANES, :]
                if n < SUBLANES:
                    cs = jnp.where(row8 < n, cs, -jnp.inf)
                cands.append(cs)
                prods.append(ps)
            cands.append(a[SUBLANES:, :] + b[0:1, :])
            prods.append(ea[SUBLANES:, :] * eb[0:1, :])
            cand = jnp.concatenate(cands, axis=0)
            prod = jnp.concatenate(prods, axis=0)
            pad = jnp.full((PEER_TOPK * SUBLANES - cand.shape[0], LANES), -jnp.inf, F32)
            tau = _sorted_top16(jnp.concatenate([cand, pad], axis=0))[PEER_TOPK - 1][0:1, :]
            z = jnp.sum(jnp.where(cand >= tau, prod, 0.0), axis=0, keepdims=True)
            n1 = jnp.zeros_like(s1)
            for j in range(4):
                n1 = n1 + jnp.where(s1 + b[j:j + 1, :] >= tau, 1.0, 0.0)
            extra = jnp.zeros_like(s1)
            for k in reversed(range(3)):
                beyond = jnp.where((a[k:k + 1, :] + b >= tau) & (row16 >= 4), 1.0, 0.0)
                extra = jnp.where(s1 == a[k:k + 1, :], jnp.sum(beyond, axis=0, keepdims=True), extra)
            n1 = n1 + extra
            n1_ref[h, :, lanes] = n1
            e1_ref[h, :, lanes] = 0.5 * jnp.exp(s1 - a[0:1, :])
            r2 = pltpu.bitcast(rank2.astype(BF16), jnp.int32)
            f2 = pltpu.bitcast((jnp.exp(s2 - b[0:1, :]) / z).astype(BF16), jnp.int32)
            for k in range(N_KEYS // BF16_ROWS):
                ks = slice(k * SUBLANES, (k + 1) * SUBLANES)
                rf_ref[h, lt, 2 * k] = r2[ks, :]
                rf_ref[h, lt, 2 * k + 1] = f2[ks, :]
        return carry

    lax.fori_loop(0, PEER_HEADS, per_head, 0)


def _select(qp_t, keys):
    NQ, T = qp_t.shape
    H = PEER_HEADS
    rows = pl.BlockSpec((H, N_KEYS, TOK_BLK), lambda i: (0, 0, i))
    rows_shape = jax.ShapeDtypeStruct((H, N_KEYS, T), F32)
    n_vregs = 2 * (N_KEYS // BF16_ROWS)
    tiles = pl.BlockSpec((H, TOK_BLK // LANES, n_vregs, SUBLANES, LANES), lambda i: (0, i, 0, 0, 0))
    tiles_shape = jax.ShapeDtypeStruct((H, T // LANES, n_vregs, SUBLANES, LANES), jnp.int32)
    return pl.pallas_call(
        _select_kernel,
        grid=(T // TOK_BLK,),
        in_specs=[pl.BlockSpec((NQ, TOK_BLK), lambda i: (0, i)),
                  pl.BlockSpec(keys.shape, lambda i: (0, 0, 0))],
        out_specs=[rows, rows, tiles],
        out_shape=[rows_shape, rows_shape, tiles_shape],
        scratch_shapes=[pltpu.VMEM((2 * H, N_KEYS, TOK_BLK), F32),
                        pltpu.VMEM((PEER_TOPK, LANES), F32),
                        pltpu.VMEM((PEER_TOPK, LANES), F32)],
        compiler_params=_cparams(("arbitrary",)),
        name="select",
    )(qp_t, keys)


def _expert_kernel(h2t_ref, u_ref, vt_ref, n1_ref, e1_ref, rf_ref,
                   x1_ref, mod_ref, fg_ref, o_ref, acc_ref, a_ref, p_ref, *w_refs):
    c = pl.program_id(1)
    EC, TB = a_ref.shape

    @pl.when(c == 0)
    def _():
        acc_ref[...] = jnp.zeros_like(acc_ref)

    groups = [slice(g * EXPERT_GRP, (g + 1) * EXPERT_GRP) for g in range(EC // EXPERT_GRP)]
    for grp in groups:
        words = slice(grp.start // 2, grp.stop // 2)
        a_ref[grp, :] = jnp.dot(pltpu.bitcast(u_ref[words, :], BF16), h2t_ref[...],
                                preferred_element_type=F32)

    zero = jnp.minimum(pl.program_id(1), 0)
    keys_per_grp = EXPERT_GRP // N_KEYS
    tiles_per_key = N_KEYS // BF16_ROWS

    def w_rows(jj, k):
        return pl.ds(pl.multiple_of(zero + (jj * N_KEYS + k * BF16_ROWS), BF16_ROWS), BF16_ROWS)

    for g, w_ref in enumerate(w_refs):
        for jj in range(keys_per_grp):
            j = g * keys_per_grp + jj
            for lt in range(TB // LANES):
                lanes = slice(lt * LANES, (lt + 1) * LANES)
                w = [None] * tiles_per_key
                for h in range(PEER_HEADS):
                    n1 = jnp.broadcast_to(n1_ref[h, j:j + 1, lanes], (BF16_ROWS, LANES)).astype(BF16)
                    e1 = jnp.broadcast_to(e1_ref[h, j:j + 1, lanes], (BF16_ROWS, LANES)).astype(BF16)
                    for k in range(tiles_per_key):
                        hit = pltpu.bitcast(rf_ref[h, lt, 2 * k], BF16) < n1
                        f2 = pltpu.bitcast(rf_ref[h, lt, 2 * k + 1], BF16)
                        term = jnp.where(hit, f2 * e1, jnp.zeros_like(e1))
                        w[k] = term if w[k] is None else w[k] + term
                for k in range(tiles_per_key):
                    w_ref[w_rows(jj, k), lanes] = w[k]

    for g, w_ref in enumerate(w_refs):
        for jj in range(keys_per_grp):
            j = g * keys_per_grp + jj
            for lt in range(TB // LANES):
                lanes = slice(lt * LANES, (lt + 1) * LANES)
                for k in range(tiles_per_key):
                    rows = slice(j * N_KEYS + k * BF16_ROWS, j * N_KEYS + (k + 1) * BF16_ROWS)
                    a = a_ref[rows, lanes]
                    gelu2 = a * (1.0 + lax.erf(a * INV_SQRT2))
                    p_ref[rows, lanes] = w_ref[w_rows(jj, k), lanes] * gelu2.astype(BF16)

    update = None
    for grp in groups:
        part = jnp.dot(pltpu.bitcast(vt_ref[:, grp], BF16), p_ref[grp, :], preferred_element_type=F32)
        update = part if update is None else update + part
    acc_ref[...] += update

    @pl.when(c == pl.num_programs(1) - 1)
    def _():
        x2 = x1_ref[...] + mod_ref[5:6, :] * acc_ref[...].T
        o_ref[...] = _rms(x2) * fg_ref[...]


def _pack_rows_kernel(x_ref, o_ref):
    o_ref[...] = pltpu.bitcast(x_ref[...].astype(BF16), jnp.int32)


def _pack_transposed_kernel(x_ref, o_ref):
    o_ref[...] = pltpu.bitcast(x_ref[...].T.astype(BF16), jnp.int32)


def _pack_expert_tables(u, v):
    E, D = u.shape
    u_words = pl.pallas_call(
        _pack_rows_kernel,
        grid=(E // TOK_BLK,),
        in_specs=[pl.BlockSpec((TOK_BLK, D), lambda g: (g, 0))],
        out_specs=pl.BlockSpec((TOK_BLK // 2, D), lambda g: (g, 0)),
        out_shape=jax.ShapeDtypeStruct((E // 2, D), jnp.int32),
        compiler_params=_cparams(("arbitrary",)),
        name="pack_u",
    )(u)
    vt_words = pl.pallas_call(
        _pack_transposed_kernel,
        grid=(E // TOK_BLK,),
        in_specs=[pl.BlockSpec((TOK_BLK, D), lambda g: (g, 0))],
        out_specs=pl.BlockSpec((D // 2, TOK_BLK), lambda g: (0, g)),
        out_shape=jax.ShapeDtypeStruct((D // 2, E), jnp.int32),
        compiler_params=_cparams(("arbitrary",)),
        name="pack_vt",
    )(v)
    return u_words, vt_words


def _experts(h2_t, u_words, vt_words, n1, e1, rf, x1, mod3, final_g, S):
    D, T = h2_t.shape
    E = vt_words.shape[1]
    H = PEER_HEADS
    TB, EC = TOK_BLK, EXPERT_BLK
    per_b = S // TB
    rows = pl.BlockSpec((H, EC // N_KEYS, TB), lambda i, c: (0, c, i))
    tiles = pl.BlockSpec((H, TB // LANES) + rf.shape[2:], lambda i, c: (0, i, 0, 0, 0))
    return pl.pallas_call(
        _expert_kernel,
        grid=(T // TB, E // EC),
        in_specs=[pl.BlockSpec((D, TB), lambda i, c: (0, i)),
                  pl.BlockSpec((EC // 2, D), lambda i, c: (c, 0)),
                  pl.BlockSpec((D // 2, EC), lambda i, c: (0, c)),
                  rows, rows, tiles,
                  pl.BlockSpec((TB, D), lambda i, c: (i, 0)),
                  pl.BlockSpec((None, 6, D), lambda i, c: (i // per_b, 0, 0)),
                  pl.BlockSpec((1, D), lambda i, c: (0, 0))],
        out_specs=pl.BlockSpec((TB, D), lambda i, c: (i, 0)),
        out_shape=jax.ShapeDtypeStruct((T, D), F32),
        scratch_shapes=[pltpu.VMEM((D, TB), F32),
                        pltpu.VMEM((EC, TB), F32),
                        pltpu.VMEM((EC, TB), BF16)]
                       + [pltpu.VMEM((EXPERT_GRP, TB), BF16)] * (EC // EXPERT_GRP),
        compiler_params=_cparams(("arbitrary", "arbitrary")),
        name="experts",
    )(h2_t, u_words, vt_words, n1, e1, rf, x1, mod3, final_g)


def kernel(x, c, ada_w, ada_b, norm1_g, w_in, conv_w, lam_q1, lam_k1, lam_q2, lam_k2, attn_norm_g, conv_norm_g, w_o, norm2_g, peer_wq, peer_keys, peer_u, peer_v, final_g):
    B, S, D = x.shape
    T = B * S
    assert ada_w.shape[0] == 1, "single-layer block"
    assert S % TOK_BLK == 0 and S % Q_BLK == 0 and Q_BLK % CHUNK == 0
    assert peer_u.shape[1] % EXPERT_BLK == 0 and (EXPERT_BLK // N_KEYS) % SUBLANES == 0

    x2 = x.reshape(T, D)
    mod3 = _modulation(c, ada_w[0], ada_b).reshape(B, 6, D)

    qkv, cbh = _inproj(x2, mod3, norm1_g, w_in[0].astype(BF16), S)

    lam_vecs = jnp.concatenate([lam_q1, lam_k1, lam_q2, lam_k2], axis=0)
    att = _attention(qkv.reshape(B, S, -1), lam_vecs, attn_norm_g)

    x1, h2_t, qp_t = _mix(x2, att.reshape(T, ATT_WIDTH), cbh, conv_w[0], conv_norm_g,
                          w_o[0].astype(BF16), mod3, norm2_g, peer_wq[0].T.astype(BF16), S)

    keys = peer_keys[0].reshape(2 * PEER_HEADS, N_KEYS, -1).astype(BF16)
    n1, e1, rf = _select(qp_t, keys)

    u_words, vt_words = _pack_expert_tables(peer_u[0], peer_v[0])
    out = _experts(h2_t, u_words, vt_words, n1, e1, rf, x1, mod3, final_g.reshape(1, D), S)
    return out.reshape(B, S, D)
```

```python
import functools
import math

import jax
import jax.numpy as jnp
from jax import lax
from jax.experimental import pallas as pl
from jax.experimental.pallas import tpu as pltpu

F32 = jnp.float32
BF16 = jnp.bfloat16

EPS = 1e-6
CHUNK = 64
ATT_HEADS = 4
ATT_HEAD_DIM = 64
ATT_V_DIM = 128
ATT_WIDTH = ATT_HEADS * ATT_V_DIM
CONV_K = 3
PEER_HEADS = 8
N_KEYS = 128
PEER_TOPK = 16
LAM_INIT = 0.8 - 0.6 * math.exp(0.0)
INV_SQRT2 = 0.7071067811865476

LANES = 128
SUBLANES = 8
BF16_ROWS = 2 * SUBLANES
Q_BLK = 256
TOK_BLK = 512
EXPERT_BLK = 2048
EXPERT_GRP = 256
VMEM_LIMIT = 48 * 1024 * 1024


def _cparams(sem):
    return pltpu.CompilerParams(dimension_semantics=sem, vmem_limit_bytes=VMEM_LIMIT)


def _rms(x):
    return x * lax.rsqrt(jnp.mean(x * x, axis=-1, keepdims=True) + EPS)


def _mod_kernel(c_ref, w_ref, b_ref, o_ref):
    sc = jax.nn.silu(c_ref[...])
    o_ref[...] = jnp.dot(sc.astype(BF16), w_ref[...].astype(BF16),
                         preferred_element_type=F32) + b_ref[...]


def _modulation(c, ada_w, ada_b):
    B, D = c.shape
    N = ada_w.shape[1]
    return pl.pallas_call(
        _mod_kernel,
        grid=(N // D,),
        in_specs=[pl.BlockSpec((B, D), lambda j: (0, 0)),
                  pl.BlockSpec((D, D), lambda j: (0, j)),
                  pl.BlockSpec((1, D), lambda j: (0, j))],
        out_specs=pl.BlockSpec((B, D), lambda j: (0, j)),
        out_shape=jax.ShapeDtypeStruct((B, N), F32),
        compiler_params=_cparams(("arbitrary",)),
        name="mod",
    )(c, ada_w, ada_b)


def _inproj_kernel(x_ref, mod_ref, g_ref, w_ref, qkv_ref, cbh_ref):
    y = _rms(x_ref[...])
    h = (y * g_ref[...]) * (1.0 + mod_ref[1:2, :]) + mod_ref[0:1, :]
    p = jnp.dot(h.astype(BF16), w_ref[...], preferred_element_type=F32)
    n_att = qkv_ref.shape[1]
    qkv_ref[...] = p[:, :n_att].astype(BF16)
    cbh_ref[...] = p[:, n_att:]


def _inproj(x2, mod3, g, w, S):
    T, D = x2.shape
    n_att = 3 * ATT_WIDTH
    n_conv = w.shape[1] - n_att
    per_b = S // TOK_BLK
    return pl.pallas_call(
        _inproj_kernel,
        grid=(T // TOK_BLK,),
        in_specs=[pl.BlockSpec((TOK_BLK, D), lambda i: (i, 0)),
                  pl.BlockSpec((None, 6, D), lambda i: (i // per_b, 0, 0)),
                  pl.BlockSpec((1, D), lambda i: (0, 0)),
                  pl.BlockSpec(w.shape, lambda i: (0, 0))],
        out_specs=[pl.BlockSpec((TOK_BLK, n_att), lambda i: (i, 0)),
                   pl.BlockSpec((TOK_BLK, n_conv), lambda i: (i, 0))],
        out_shape=[jax.ShapeDtypeStruct((T, n_att), BF16),
                   jax.ShapeDtypeStruct((T, n_conv), F32)],
        compiler_params=_cparams(("arbitrary",)),
        name="inproj",
    )(x2, mod3, g, w)


def _attn_kernel(q_ref, k_ref, v_ref, lam_ref, g_ref, o_ref, bias_ref, vext_ref):
    S = q_ref.shape[0]
    h = pl.program_id(1)
    vext_ref[:, :ATT_V_DIM] = v_ref[...]
    vext_ref[:, ATT_V_DIM:] = jnp.ones((S, ATT_V_DIM), BF16)

    @pl.when(pl.program_id(0) == 0)
    def _():
        slope = jnp.where(h == 0, 2.0 ** -2,
                          jnp.where(h == 1, 2.0 ** -4,
                                    jnp.where(h == 2, 2.0 ** -6, 2.0 ** -8))).astype(F32)
        r = lax.broadcasted_iota(jnp.int32, (Q_BLK, S), 0) + (S - Q_BLK)
        x = lax.broadcasted_iota(jnp.int32, (Q_BLK, S), 1)
        bias = -slope * jnp.abs(r - x).astype(F32)
        allowed = (x // CHUNK) <= (r // CHUNK)
        bias_ref[h] = jnp.where(allowed, bias, -jnp.inf)

    lv = lam_ref[...]
    lam = (jnp.exp(jnp.sum(lv[0:1, :] * lv[1:2, :], axis=-1, keepdims=True))
           - jnp.exp(jnp.sum(lv[2:3, :] * lv[3:4, :], axis=-1, keepdims=True))
           + LAM_INIT)
    gain = g_ref[...] * (1.0 - LAM_INIT)
    nt = (((1,), (1,)), ((), ()))
    for j in range(S // Q_BLK):
        q0 = j * Q_BLK
        L = q0 + Q_BLK
        qb = q_ref[q0:q0 + Q_BLK, :]
        lane = lax.broadcasted_iota(jnp.int32, qb.shape, 1)
        scale = jnp.asarray(ATT_HEAD_DIM ** -0.5, BF16)
        zero = jnp.zeros_like(qb)
        q_lo = jnp.where(lane < ATT_HEAD_DIM, qb, zero) * scale
        q_hi = jnp.where(lane >= ATT_HEAD_DIM, qb, zero) * scale
        kk = k_ref[0:L, :]
        vv = vext_ref[0:L, :]

        bias = bias_ref[h, :, S - L:S]
        s = lax.dot_general(jnp.concatenate([q_lo, q_hi], axis=0), kk, nt, preferred_element_type=F32)
        s = s + jnp.concatenate([bias, bias], axis=0)
        m = jnp.max(s, axis=-1, keepdims=True)
        p = jnp.exp(s - m)
        ov = jnp.dot(p.astype(BF16), vv, preferred_element_type=F32)
        on = ov[:, :ATT_V_DIM] / ov[:, ATT_V_DIM:ATT_V_DIM + 1]
        o = on[:Q_BLK, :] - lam * on[Q_BLK:, :]
        o_ref[q0:q0 + Q_BLK, :] = (_rms(o) * gain).astype(o_ref.dtype)


def _attention(qkv3, lam_vecs, attn_g):
    B, S, _ = qkv3.shape
    dv = ATT_V_DIM
    blk = lambda off: pl.BlockSpec((None, S, dv), lambda b, h: (b, 0, off + h))
    return pl.pallas_call(
        _attn_kernel,
        grid=(B, ATT_HEADS),
        in_specs=[blk(0), blk(ATT_HEADS), blk(2 * ATT_HEADS),
                  pl.BlockSpec(lam_vecs.shape, lambda b, h: (0, 0)),
                  pl.BlockSpec((1, dv), lambda b, h: (0, h))],
        out_specs=pl.BlockSpec((None, S, dv), lambda b, h: (b, 0, h)),
        out_shape=jax.ShapeDtypeStruct((B, S, ATT_WIDTH), BF16),
        scratch_shapes=[pltpu.VMEM((ATT_HEADS, Q_BLK, S), F32),
                        pltpu.VMEM((S, 2 * ATT_V_DIM), BF16)],
        compiler_params=_cparams(("arbitrary", "arbitrary")),
        name="attn",
    )(qkv3, qkv3, qkv3, lam_vecs, attn_g)


def _mix_kernel(x_ref, att_ref, cb_ref, cc_ref, ch_ref, ccp_ref, chp_ref, cw_ref, cg_ref,
                wo_ref, mod_ref, g2_ref, wqt_ref, x1_ref, h2t_ref, qpt_ref, *, blocks_per_seq):
    i = pl.program_id(0)
    u = cc_ref[...] * ch_ref[...]
    first = (i % blocks_per_seq) == 0
    up = jnp.where(first, 0.0, ccp_ref[...] * chp_ref[...])
    row = lax.broadcasted_iota(jnp.int32, up.shape, 0)

    def shifted(k):
        uk = pltpu.roll(u, k, 0)
        top = jnp.where(row < k, pltpu.roll(up, k, 0), uk[0:SUBLANES, :])
        return jnp.concatenate([top, uk[SUBLANES:, :]], axis=0)

    cw = cw_ref[...]
    y = cw[2:3, :] * u + cw[1:2, :] * shifted(1) + cw[0:1, :] * shifted(2)
    cv = _rms(cb_ref[...] * y) * cg_ref[...]
    both = jnp.concatenate([att_ref[...], cv.astype(BF16)], axis=-1)
    mix = jnp.dot(both, wo_ref[...], preferred_element_type=F32)
    x1 = x_ref[...] + mod_ref[2:3, :] * mix
    x1_ref[...] = x1
    h2 = (_rms(x1) * g2_ref[...]) * (1.0 + mod_ref[4:5, :]) + mod_ref[3:4, :]
    h2t = h2.T.astype(BF16)
    h2t_ref[...] = h2t
    qpt_ref[...] = jnp.dot(wqt_ref[...], h2t, preferred_element_type=F32).astype(BF16)


def _mix(x2, att2, cbh, conv_w, conv_g, w_o, mod3, g2, wq_t, S):
    T, D = x2.shape
    C = conv_w.shape[1]
    NQ = wq_t.shape[0]
    per_b = S // TOK_BLK
    rows8 = TOK_BLK // SUBLANES
    prev = lambda col: pl.BlockSpec((SUBLANES, C), lambda i: (jnp.maximum(i * rows8 - 1, 0), col))
    full = lambda a: pl.BlockSpec(a.shape, lambda i: (0,) * a.ndim)
    return pl.pallas_call(
        functools.partial(_mix_kernel, blocks_per_seq=per_b),
        grid=(T // TOK_BLK,),
        in_specs=[pl.BlockSpec((TOK_BLK, D), lambda i: (i, 0)),
                  pl.BlockSpec((TOK_BLK, ATT_WIDTH), lambda i: (i, 0)),
                  pl.BlockSpec((TOK_BLK, C), lambda i: (i, 0)),
                  pl.BlockSpec((TOK_BLK, C), lambda i: (i, 1)),
                  pl.BlockSpec((TOK_BLK, C), lambda i: (i, 2)),
                  prev(1), prev(2),
                  full(conv_w), full(conv_g), full(w_o),
                  pl.BlockSpec((None, 6, D), lambda i: (i // per_b, 0, 0)),
                  full(g2), full(wq_t)],
        out_specs=[pl.BlockSpec((TOK_BLK, D), lambda i: (i, 0)),
                   pl.BlockSpec((D, TOK_BLK), lambda i: (0, i)),
                   pl.BlockSpec((NQ, TOK_BLK), lambda i: (0, i))],
        out_shape=[jax.ShapeDtypeStruct((T, D), F32),
                   jax.ShapeDtypeStruct((D, T), BF16),
                   jax.ShapeDtypeStruct((NQ, T), BF16)],
        compiler_params=_cparams(("arbitrary",)),
        name="mix",
    )(x2, att2, cbh, cbh, cbh, cbh, cbh, conv_w, conv_g, w_o, mod3, g2, wq_t)


_PAIR_ROWS = ((0, 8, 0), (0, 8, 8), (1, 8, 0), (2, 5, 0), (3, 4, 0), (4, 3, 0),
              (5, 2, 0), (6, 2, 0), (7, 2, 0))


def _batcher_pairs(n):
    pairs = []
    p = 1
    while p < n:
        k = p
        while k >= 1:
            for j in range(k % p, n - k, 2 * k):
                for i in range(min(k, n - j - k)):
                    if (i + j) // (2 * p) == (i + j + k) // (2 * p):
                        pairs.append((i + j, i + j + k))
            k //= 2
        p *= 2
    return pairs


def _sorted_top16(x):
    v = [x[i * SUBLANES:(i + 1) * SUBLANES, :] for i in range(PEER_TOPK)]

    def exchange(i, j):
        v[i], v[j] = jnp.maximum(v[i], v[j]), jnp.minimum(v[i], v[j])

    for i, j in _batcher_pairs(PEER_TOPK):
        exchange(i, j)
    shift = SUBLANES // 2
    while shift >= 1:
        other = [pltpu.roll(t, shift, 0) for t in v]
        v = [jnp.maximum(v[i], other[PEER_TOPK - 1 - i]) for i in range(PEER_TOPK)]
        d = PEER_TOPK // 2
        while d >= 1:
            for i in range(PEER_TOPK):
                if i & d == 0:
                    exchange(i, i + d)
            d //= 2
        shift //= 2
    return v


def _rank_in_sorted(top, x):
    def sel(bits, options):
        if not bits:
            return options[0]
        half = len(options) // 2
        return jnp.where(bits[0], sel(bits[1:], options[half:]), sel(bits[1:], options[:half]))

    c8 = top[7] > x
    c4 = sel([c8], [top[3], top[11]]) > x
    c2 = sel([c8, c4], [top[1], top[5], top[9], top[13]]) > x
    c1 = sel([c8, c4, c2], [top[2 * i] for i in range(8)]) > x
    c16 = top[PEER_TOPK - 1] > x
    rank = jnp.where(c8, 8.0, 0.0)
    for bit, weight in ((c4, 4.0), (c2, 2.0), (c1, 1.0), (c16, 1.0)):
        rank = rank + jnp.where(bit, weight, 0.0)
    return rank


def _select_kernel(qpt_ref, keys_ref, n1_ref, e1_ref, rf_ref, sc_ref, a_ref, b_ref):
    TB = qpt_ref.shape[1]
    for hp in range(2 * PEER_HEADS):
        sc_ref[hp] = jnp.dot(keys_ref[hp], qpt_ref[hp * N_KEYS:(hp + 1) * N_KEYS, :],
                             preferred_element_type=F32)

    row8 = lax.broadcasted_iota(jnp.int32, (SUBLANES, LANES), 0)
    row16 = lax.broadcasted_iota(jnp.int32, (PEER_TOPK, LANES), 0)

    def per_head(h, carry):
        for lt in range(TB // LANES):
            lanes = slice(lt * LANES, (lt + 1) * LANES)
            s1 = sc_ref[2 * h, :, lanes]
            s2 = sc_ref[2 * h + 1, :, lanes]
            top1 = _sorted_top16(s1)
            top2 = _sorted_top16(s2)
            for r in range(PEER_TOPK):
                a_ref[r:r + 1, :] = top1[r][0:1, :]
                b_ref[r:r + 1, :] = top2[r][0:1, :]
            rank2 = jnp.concatenate([_rank_in_sorted(top2, s2[i * SUBLANES:(i + 1) * SUBLANES, :])
                                     for i in range(N_KEYS // SUBLANES)], axis=0)
            a = a_ref[...]
            b = b_ref[...]
            ea = jnp.exp(a - a[0:1, :])
            eb = jnp.exp(b - b[0:1, :])
            cands, prods = [], []
            for (i, n, j0) in _PAIR_ROWS:
                cs = a[i:i + 1, :] + b[j0:j0 + SUBLANES, :]
                ps = ea[i:i + 1, :] * eb[j0:j0 + SUBLANES, :]
                if n < SUBLANES:
                    cs = jnp.where(row8 < n, cs, -jnp.inf)
                cands.append(cs)
                prods.append(ps)
            cands.append(a[SUBLANES:, :] + b[0:1, :])
            prods.append(ea[SUBLANES:, :] * eb[0:1, :])
            cand = jnp.concatenate(cands, axis=0)
            prod = jnp.concatenate(prods, axis=0)
            pad = jnp.full((PEER_TOPK * SUBLANES - cand.shape[0], LANES), -jnp.inf, F32)
            tau = _sorted_top16(jnp.concatenate([cand, pad], axis=0))[PEER_TOPK - 1][0:1, :]
            z = jnp.sum(jnp.where(cand >= tau, prod, 0.0), axis=0, keepdims=True)
            n1 = jnp.zeros_like(s1)
            for j in range(4):
                n1 = n1 + jnp.where(s1 + b[j:j + 1, :] >= tau, 1.0, 0.0)
            extra = jnp.zeros_like(s1)
            for k in reversed(range(3)):
                beyond = jnp.where((a[k:k + 1, :] + b >= tau) & (row16 >= 4), 1.0, 0.0)
                extra = jnp.where(s1 == a[k:k + 1, :], jnp.sum(beyond, axis=0, keepdims=True), extra)
            n1 = n1 + extra
            n1_ref[h, :, lanes] = n1
            e1_ref[h, :, lanes] = 0.5 * jnp.exp(s1 - a[0:1, :])
            r2 = pltpu.bitcast(rank2.astype(BF16), jnp.int32)
            f2 = pltpu.bitcast((jnp.exp(s2 - b[0:1, :]) / z).astype(BF16), jnp.int32)
            for k in range(N_KEYS // BF16_ROWS):
                ks = slice(k * SUBLANES, (k + 1) * SUBLANES)
                rf_ref[h, lt, 2 * k] = r2[ks, :]
                rf_ref[h, lt, 2 * k + 1] = f2[ks, :]
        return carry

    lax.fori_loop(0, PEER_HEADS, per_head, 0)


def _select(qp_t, keys):
    NQ, T = qp_t.shape
    H = PEER_HEADS
    rows = pl.BlockSpec((H, N_KEYS, TOK_BLK), lambda i: (0, 0, i))
    rows_shape = jax.ShapeDtypeStruct((H, N_KEYS, T), F32)
    n_vregs = 2 * (N_KEYS // BF16_ROWS)
    tiles = pl.BlockSpec((H, TOK_BLK // LANES, n_vregs, SUBLANES, LANES), lambda i: (0, i, 0, 0, 0))
    tiles_shape = jax.ShapeDtypeStruct((H, T // LANES, n_vregs, SUBLANES, LANES), jnp.int32)
    return pl.pallas_call(
        _select_kernel,
        grid=(T // TOK_BLK,),
        in_specs=[pl.BlockSpec((NQ, TOK_BLK), lambda i: (0, i)),
                  pl.BlockSpec(keys.shape, lambda i: (0, 0, 0))],
        out_specs=[rows, rows, tiles],
        out_shape=[rows_shape, rows_shape, tiles_shape],
        scratch_shapes=[pltpu.VMEM((2 * H, N_KEYS, TOK_BLK), F32),
                        pltpu.VMEM((PEER_TOPK, LANES), F32),
                        pltpu.VMEM((PEER_TOPK, LANES), F32)],
        compiler_params=_cparams(("arbitrary",)),
        name="select",
    )(qp_t, keys)


def _expert_kernel(h2t_ref, u_ref, vt_ref, n1_ref, e1_ref, rf_ref,
                   x1_ref, mod_ref, fg_ref, o_ref, acc_ref, a_ref, p_ref, *w_refs):
    c = pl.program_id(1)
    EC, TB = a_ref.shape

    @pl.when(c == 0)
    def _():
        acc_ref[...] = jnp.zeros_like(acc_ref)

    groups = [slice(g * EXPERT_GRP, (g + 1) * EXPERT_GRP) for g in range(EC // EXPERT_GRP)]
    for grp in groups:
        words = slice(grp.start // 2, grp.stop // 2)
        a_ref[grp, :] = jnp.dot(pltpu.bitcast(u_ref[words, :], BF16), h2t_ref[...],
                                preferred_element_type=F32)

    zero = jnp.minimum(pl.program_id(1), 0)
    keys_per_grp = EXPERT_GRP // N_KEYS
    tiles_per_key = N_KEYS // BF16_ROWS

    def w_rows(jj, k):
        return pl.ds(pl.multiple_of(zero + (jj * N_KEYS + k * BF16_ROWS), BF16_ROWS), BF16_ROWS)

    for g, w_ref in enumerate(w_refs):
        for lt in range(TB // LANES):
            lanes = slice(lt * LANES, (lt + 1) * LANES)
            w = [[None] * tiles_per_key for _ in range(keys_per_grp)]
            for h in range(PEER_HEADS):
                n1, e1 = [], []
                for jj in range(keys_per_grp):
                    j = g * keys_per_grp + jj
                    n1.append(jnp.broadcast_to(n1_ref[h, j:j + 1, lanes], (BF16_ROWS, LANES)).astype(BF16))
                    e1.append(jnp.broadcast_to(e1_ref[h, j:j + 1, lanes], (BF16_ROWS, LANES)).astype(BF16))
                for k in range(tiles_per_key):
                    r2 = pltpu.bitcast(rf_ref[h, lt, 2 * k], BF16)
                    f2 = pltpu.bitcast(rf_ref[h, lt, 2 * k + 1], BF16)
                    for jj in range(keys_per_grp):
                        term = jnp.where(r2 < n1[jj], f2 * e1[jj], jnp.zeros_like(f2))
                        w[jj][k] = term if w[jj][k] is None else w[jj][k] + term
            for jj in range(keys_per_grp):
                for k in range(tiles_per_key):
                    w_ref[w_rows(jj, k), lanes] = w[jj][k]

    for g, w_ref in enumerate(w_refs):
        for jj in range(keys_per_grp):
            j = g * keys_per_grp + jj
            for lt in range(TB // LANES):
                lanes = slice(lt * LANES, (lt + 1) * LANES)
                for k in range(tiles_per_key):
                    rows = slice(j * N_KEYS + k * BF16_ROWS, j * N_KEYS + (k + 1) * BF16_ROWS)
                    a = a_ref[rows, lanes]
                    gelu2 = a * (1.0 + lax.erf(a * INV_SQRT2))
                    p_ref[rows, lanes] = w_ref[w_rows(jj, k), lanes] * gelu2.astype(BF16)

    update = None
    for grp in groups:
        part = jnp.dot(pltpu.bitcast(vt_ref[:, grp], BF16), p_ref[grp, :], preferred_element_type=F32)
        update = part if update is None else update + part
    acc_ref[...] += update

    @pl.when(c == pl.num_programs(1) - 1)
    def _():
        x2 = x1_ref[...] + mod_ref[5:6, :] * acc_ref[...].T
        o_ref[...] = _rms(x2) * fg_ref[...]


def _pack_rows_kernel(x_ref, o_ref):
    o_ref[...] = pltpu.bitcast(x_ref[...].astype(BF16), jnp.int32)


def _pack_transposed_kernel(x_ref, o_ref):
    o_ref[...] = pltpu.bitcast(x_ref[...].T.astype(BF16), jnp.int32)


def _pack_expert_tables(u, v):
    E, D = u.shape
    u_words = pl.pallas_call(
        _pack_rows_kernel,
        grid=(E // TOK_BLK,),
        in_specs=[pl.BlockSpec((TOK_BLK, D), lambda g: (g, 0))],
        out_specs=pl.BlockSpec((TOK_BLK // 2, D), lambda g: (g, 0)),
        out_shape=jax.ShapeDtypeStruct((E // 2, D), jnp.int32),
        compiler_params=_cparams(("arbitrary",)),
        name="pack_u",
    )(u)
    vt_words = pl.pallas_call(
        _pack_transposed_kernel,
        grid=(E // TOK_BLK,),
        in_specs=[pl.BlockSpec((TOK_BLK, D), lambda g: (g, 0))],
        out_specs=pl.BlockSpec((D // 2, TOK_BLK), lambda g: (0, g)),
        out_shape=jax.ShapeDtypeStruct((D // 2, E), jnp.int32),
        compiler_params=_cparams(("arbitrary",)),
        name="pack_vt",
    )(v)
    return u_words, vt_words


def _experts(h2_t, u_words, vt_words, n1, e1, rf, x1, mod3, final_g, S):
    D, T = h2_t.shape
    E = vt_words.shape[1]
    H = PEER_HEADS
    TB, EC = TOK_BLK, EXPERT_BLK
    per_b = S // TB
    rows = pl.BlockSpec((H, EC // N_KEYS, TB), lambda i, c: (0, c, i))
    tiles = pl.BlockSpec((H, TB // LANES) + rf.shape[2:], lambda i, c: (0, i, 0, 0, 0))
    return pl.pallas_call(
        _expert_kernel,
        grid=(T // TB, E // EC),
        in_specs=[pl.BlockSpec((D, TB), lambda i, c: (0, i)),
                  pl.BlockSpec((EC // 2, D), lambda i, c: (c, 0)),
                  pl.BlockSpec((D // 2, EC), lambda i, c: (0, c)),
                  rows, rows, tiles,
                  pl.BlockSpec((TB, D), lambda i, c: (i, 0)),
                  pl.BlockSpec((None, 6, D), lambda i, c: (i // per_b, 0, 0)),
                  pl.BlockSpec((1, D), lambda i, c: (0, 0))],
        out_specs=pl.BlockSpec((TB, D), lambda i, c: (i, 0)),
        out_shape=jax.ShapeDtypeStruct((T, D), F32),
        scratch_shapes=[pltpu.VMEM((D, TB), F32),
                        pltpu.VMEM((EC, TB), F32),
                        pltpu.VMEM((EC, TB), BF16)]
                       + [pltpu.VMEM((EXPERT_GRP, TB), BF16)] * (EC // EXPERT_GRP),
        compiler_params=_cparams(("arbitrary", "arbitrary")),
        name="experts",
    )(h2_t, u_words, vt_words, n1, e1, rf, x1, mod3, final_g)


def kernel(x, c, ada_w, ada_b, norm1_g, w_in, conv_w, lam_q1, lam_k1, lam_q2, lam_k2, attn_norm_g, conv_norm_g, w_o, norm2_g, peer_wq, peer_keys, peer_u, peer_v, final_g):
    B, S, D = x.shape
    T = B * S
    assert ada_w.shape[0] == 1, "single-layer block"
    assert S % TOK_BLK == 0 and S % Q_BLK == 0 and Q_BLK % CHUNK == 0
    assert peer_u.shape[1] % EXPERT_BLK == 0 and (EXPERT_BLK // N_KEYS) % SUBLANES == 0

    x2 = x.reshape(T, D)
    mod3 = _modulation(c, ada_w[0], ada_b).reshape(B, 6, D)

    qkv, cbh = _inproj(x2, mod3, norm1_g, w_in[0].astype(BF16), S)

    lam_vecs = jnp.concatenate([lam_q1, lam_k1, lam_q2, lam_k2], axis=0)
    att = _attention(qkv.reshape(B, S, -1), lam_vecs, attn_norm_g)

    x1, h2_t, qp_t = _mix(x2, att.reshape(T, ATT_WIDTH), cbh, conv_w[0], conv_norm_g,
                          w_o[0].astype(BF16), mod3, norm2_g, peer_wq[0].T.astype(BF16), S)

    keys = peer_keys[0].reshape(2 * PEER_HEADS, N_KEYS, -1).astype(BF16)
    n1, e1, rf = _select(qp_t, keys)

    u_words, vt_words = _pack_expert_tables(peer_u[0], peer_v[0])
    out = _experts(h2_t, u_words, vt_words, n1, e1, rf, x1, mod3, final_g.reshape(1, D), S)
    return out.reshape(B, S, D)
```
